```python
import jax, jax.numpy as jnp
from jax import lax
import numpy as np

D_MODEL = 1024
BATCH = 16
SEQ = 2048
DEPTH = 4

N_EVEN = (DEPTH + 1) // 2
N_ODD = DEPTH // 2

HEAD_DIM = 64
D_CONV_A = D_MODEL // 2
N_FOX_HEADS = (D_MODEL // 2) // HEAD_DIM
D_FOX = N_FOX_HEADS * HEAD_DIM
SHORT_CONV_WIDTH = 3
CONFORMER_CONV_WIDTH = 31
BLOCK_Q = 128
_FF_RAW = -(-8 * D_MODEL // 3)
D_FF = -(-_FF_RAW // 256) * 256
RMS_EPS = 1e-6
LN_EPS = 1e-5

OFF_GB = 0
OFF_GC = OFF_GB + D_CONV_A
OFF_XA = OFF_GC + D_CONV_A
OFF_Q = OFF_XA + D_CONV_A
OFF_K = OFF_Q + D_FOX
OFF_V = OFF_K + D_FOX
OFF_F = OFF_V + D_FOX
D_IN_EVEN = OFF_F + N_FOX_HEADS

kernel_name = "hybrid_shortconv_fox_conformer_swiglu"


def rms_norm(x, g):
    xf = x.astype(jnp.float32)
    y = xf * lax.rsqrt(jnp.mean(xf * xf, axis=-1, keepdims=True) + RMS_EPS)
    return (y * g.astype(jnp.float32)).astype(x.dtype)


def layer_norm(x, g, b):
    xf = x.astype(jnp.float32)
    mu = jnp.mean(xf, axis=-1, keepdims=True)
    xc = xf - mu
    var = jnp.mean(xc * xc, axis=-1, keepdims=True)
    y = xc * lax.rsqrt(var + LN_EPS)
    return (y * g.astype(jnp.float32) + b.astype(jnp.float32)).astype(x.dtype)


def causal_depthwise_conv(x, w):
    k_width, ch = w.shape
    return lax.conv_general_dilated(
        x, w[:, None, :].astype(x.dtype), window_strides=(1,),
        padding=[(k_width - 1, 0)], dimension_numbers=("NWC", "WIO", "NWC"),
        feature_group_count=ch)


def forgetting_attention(q, k, v, log_f):
    seq = q.shape[1]
    c = jnp.cumsum(log_f, axis=1).transpose(0, 2, 1)
    scale = HEAD_DIM ** -0.5
    outs = []
    for start in range(0, seq, BLOCK_Q):
        end = start + BLOCK_Q
        s = jnp.einsum("bqhd,bkhd->bhqk", q[:, start:end], k[:, :end],
                       preferred_element_type=jnp.float32) * scale
        bias = c[:, :, start:end, None] - c[:, :, None, :end]
        mask = (start + jnp.arange(BLOCK_Q))[:, None] >= jnp.arange(end)[None, :]
        s = jnp.where(mask, s + bias, -jnp.inf)
        p = jax.nn.softmax(s, axis=-1)
        outs.append(jnp.einsum("bhqk,bkhd->bqhd", p.astype(v.dtype), v[:, :end]))
    return jnp.concatenate(outs, axis=1)


def shortconv_fox_mixer(h, w_in, b_f, conv_w, w_out):
    bsz, seq, _ = h.shape
    proj = h @ w_in
    gate_b = proj[..., OFF_GB:OFF_GC]
    gate_c = proj[..., OFF_GC:OFF_XA]
    xa = proj[..., OFF_XA:OFF_Q]
    y_a = gate_b * causal_depthwise_conv(gate_c * xa, conv_w)
    q = proj[..., OFF_Q:OFF_K].reshape(bsz, seq, N_FOX_HEADS, HEAD_DIM)
    k = proj[..., OFF_K:OFF_V].reshape(bsz, seq, N_FOX_HEADS, HEAD_DIM)
    v = proj[..., OFF_V:OFF_F].reshape(bsz, seq, N_FOX_HEADS, HEAD_DIM)
    log_f = jax.nn.log_sigmoid((proj[..., OFF_F:] + b_f).astype(jnp.float32))
    y_b = forgetting_attention(q, k, v, log_f).reshape(bsz, seq, D_FOX)
    return jnp.concatenate([y_a, y_b.astype(h.dtype)], axis=-1) @ w_out


def conformer_conv_module(h, w_pw1, b_pw1, w_dw, b_dw, ln_g, ln_b, w_pw2, b_pw2):
    a, g = jnp.split(h @ w_pw1 + b_pw1, 2, axis=-1)
    u = a * jax.nn.sigmoid(g)
    u = causal_depthwise_conv(u, w_dw) + b_dw
    u = jax.nn.silu(layer_norm(u, ln_g, ln_b))
    return u @ w_pw2 + b_pw2


def swiglu_ffn(h, w_ffn_in, w_ffn_out):
    g, u = jnp.split(h @ w_ffn_in, 2, axis=-1)
    return (jax.nn.silu(g) * u) @ w_ffn_out


def setup_inputs(seed: int = 0) -> dict:
    key = jax.random.key(seed)
    ks = jax.random.split(key, 20)
    f32 = jnp.float32
    nrm = lambda k, shape, s: (jax.random.normal(k, shape, f32) * s)
    gain = lambda k, shape: 1.0 + 0.05 * jax.random.normal(k, shape, f32)
    return {
        "x": jax.random.normal(ks[0], (BATCH, SEQ, D_MODEL), f32),
        "norm_mix_pre": gain(ks[1], (DEPTH, D_MODEL)),
        "norm_mix_post": gain(ks[2], (DEPTH, D_MODEL)),
        "norm_ffn_pre": gain(ks[3], (DEPTH, D_MODEL)),
        "norm_ffn_post": gain(ks[4], (DEPTH, D_MODEL)),
        "w_in": nrm(ks[5], (N_EVEN, D_MODEL, D_IN_EVEN), D_MODEL ** -0.5),
        "b_forget": 1.0 + 3.0 * jax.random.uniform(ks[6], (N_EVEN, N_FOX_HEADS), f32),
        "w_short_conv": nrm(ks[7], (N_EVEN, SHORT_CONV_WIDTH, D_CONV_A), SHORT_CONV_WIDTH ** -0.5),
        "w_out": nrm(ks[8], (N_EVEN, D_CONV_A + D_FOX, D_MODEL), (D_CONV_A + D_FOX) ** -0.5),
        "w_pw1": nrm(ks[9], (N_ODD, D_MODEL, 2 * D_MODEL), D_MODEL ** -0.5),
        "b_pw1": nrm(ks[10], (N_ODD, 2 * D_MODEL), 0.02),
        "w_dw": nrm(ks[11], (N_ODD, CONFORMER_CONV_WIDTH, D_MODEL), CONFORMER_CONV_WIDTH ** -0.5),
        "b_dw": nrm(ks[12], (N_ODD, D_MODEL), 0.02),
        "ln_g": gain(ks[13], (N_ODD, D_MODEL)),
        "ln_b": nrm(ks[14], (N_ODD, D_MODEL), 0.02),
        "w_pw2": nrm(ks[15], (N_ODD, D_MODEL, D_MODEL), D_MODEL ** -0.5),
        "b_pw2": nrm(ks[16], (N_ODD, D_MODEL), 0.02),
        "w_ffn_in": nrm(ks[17], (DEPTH, D_MODEL, 2 * D_FF), D_MODEL ** -0.5),
        "w_ffn_out": nrm(ks[18], (DEPTH, D_FF, D_MODEL), D_FF ** -0.5),
    }


def reference(x, norm_mix_pre, norm_mix_post, norm_ffn_pre, norm_ffn_post,
              w_in, b_forget, w_short_conv, w_out,
              w_pw1, b_pw1, w_dw, b_dw, ln_g, ln_b, w_pw2, b_pw2,
              w_ffn_in, w_ffn_out):
    h = x
    for layer in range(DEPTH):
        j = layer // 2
        u = rms_norm(h, norm_mix_pre[layer])
        if layer % 2 == 0:
            m = shortconv_fox_mixer(u, w_in[j], b_forget[j], w_short_conv[j], w_out[j])
        else:
            m = conformer_conv_module(u, w_pw1[j], b_pw1[j], w_dw[j], b_dw[j],
                                      ln_g[j], ln_b[j], w_pw2[j], b_pw2[j])
        h = h + rms_norm(m, norm_mix_post[layer])
        f = swiglu_ffn(rms_norm(h, norm_ffn_pre[layer]), w_ffn_in[layer], w_ffn_out[layer])
        h = h + rms_norm(f, norm_ffn_post[layer])
    return h
```

```python
import functools

import jax
import jax.numpy as jnp
from jax import lax
from jax.experimental import pallas as pl
from jax.experimental.pallas import tpu as pltpu

D_MODEL = 1024
HEAD_DIM = 64
N_HEADS = 8
D_HALF = D_MODEL // 2
SHORT_K = 3
CONF_K = 31
D_FF = 2816
RMS_EPS = 1e-6
LN_EPS = 1e-5
LANES = 128
SUBLANES = 8

TM = 512
TQ = 256
TK = 256
FF_CHUNK = 256
N_FF_CHUNKS = D_FF // FF_CHUNK
CONF_HALO = 32
CONV_ROWS = 32
CONV_LANES = 256
VMEM_LIMIT = 56 * 1024 * 1024
NEG_BIG = -1e30

F32 = jnp.float32
BF16 = jnp.bfloat16


def _dot(a, b):
    return jnp.dot(a, b, preferred_element_type=F32)


def _rms(x, g):
    return x * lax.rsqrt(jnp.mean(x * x, axis=-1, keepdims=True) + RMS_EPS) * g


def _sigmoid(x):
    return 1.0 / (1.0 + jnp.exp(-x))


def _const_spec(shape):
    nd = len(shape)
    return pl.BlockSpec(shape, lambda b, t: (0,) * nd, pipeline_mode=pl.Buffered(1))


def _tile_spec(rows, cols):
    return pl.BlockSpec((1, rows, cols), lambda b, t: (b, t, 0))


def _ffn(h1, g_pre, wg_ref, wu_ref, wo_ref, g_post):
    x = _rms(h1, g_pre).astype(BF16)
    acc = jnp.zeros(h1.shape, F32)
    for c in range(N_FF_CHUNKS):
        g = _dot(x, wg_ref[c])
        u = _dot(x, wu_ref[c])
        a = (g * _sigmoid(g) * u).astype(BF16)
        acc = acc + _dot(a, wo_ref[c])
    return h1 + _rms(acc, g_post)


def _even_in_kernel(h_ref, g_ref, w_ref, wf_ref, bf_ref, cw_ref, tri_ref,
                    ya_ref, q_ref, k_ref, v_ref, c_ref, pbuf, ccarry):
    t = pl.program_id(1)

    @pl.when(t == 0)
    def _():
        pbuf[0:SUBLANES, :] = jnp.zeros((SUBLANES, D_HALF), F32)
        ccarry[...] = jnp.zeros(ccarry.shape, F32)

    x = _rms(h_ref[0], g_ref[...]).astype(BF16)

    def proj(i):
        return _dot(x, w_ref[:, i * D_HALF:(i + 1) * D_HALF])

    p = proj(1) * proj(2)
    pbuf[SUBLANES:SUBLANES + TM, :] = p
    p1 = pbuf[SUBLANES - 1:SUBLANES - 1 + TM, :]
    p2 = pbuf[SUBLANES - 2:SUBLANES - 2 + TM, :]
    conv = cw_ref[0:1, :] * p2 + cw_ref[1:2, :] * p1 + cw_ref[2:3, :] * p
    ya_ref[0] = (proj(0) * conv).astype(BF16)
    pbuf[0:SUBLANES, :] = pbuf[TM:TM + SUBLANES, :]

    q_ref[0] = (proj(3) * (HEAD_DIM ** -0.5)).astype(BF16)
    k_ref[0] = proj(4).astype(BF16)
    v_ref[0] = proj(5).astype(BF16)

    z = _dot(x, wf_ref[...]) + bf_ref[...]
    lf = jnp.minimum(z, 0.0) - jnp.log1p(jnp.exp(-jnp.abs(z)))
    hi = lf.astype(BF16)
    r1 = lf - hi.astype(F32)
    mid = r1.astype(BF16)
    lo = (r1 - mid.astype(F32)).astype(BF16)
    tri = tri_ref[...]
    c = (_dot(tri, hi) + _dot(tri, mid)) + _dot(tri, lo) + ccarry[0:1, :]
    c_ref[0] = c
    ccarry[...] = jnp.broadcast_to(c[TM - 1:TM, :], ccarry.shape)


def _even_in(h, g_pre, w_main, w_f, b_f, conv_w, tri):
    bsz, seq, _ = h.shape
    out_bf = jax.ShapeDtypeStruct((bsz, seq, D_HALF), BF16)
    return pl.pallas_call(
        _even_in_kernel,
        grid=(bsz, seq // TM),
        in_specs=[
            _tile_spec(TM, D_MODEL),
            _const_spec((1, D_MODEL)),
            _const_spec((D_MODEL, 6 * D_HALF)),
            _const_spec((D_MODEL, LANES)),
            _const_spec((1, LANES)),
            _const_spec((SHORT_K, D_HALF)),
            _const_spec((TM, TM)),
        ],
        out_specs=[_tile_spec(TM, D_HALF)] * 4 + [_tile_spec(TM, LANES)],
        out_shape=[out_bf] * 4 + [jax.ShapeDtypeStruct((bsz, seq, LANES), F32)],
        scratch_shapes=[
            pltpu.VMEM((TM + 2 * SUBLANES, D_HALF), F32),
            pltpu.VMEM((SUBLANES, LANES), F32),
        ],
        compiler_params=pltpu.CompilerParams(
            dimension_semantics=("parallel", "arbitrary"),
            vmem_limit_bytes=VMEM_LIMIT),
        name="even_in_proj",
    )(h, g_pre, w_main, w_f, b_f, conv_w, tri)


def _fox_kernel(q_ref, k_ref, v_ref, c_ref, ct_ref, o_ref):
    i = pl.program_id(1)
    row = lax.broadcasted_iota(jnp.int32, (TQ, TK), 0)
    col = lax.broadcasted_iota(jnp.int32, (TQ, TK), 1)
    causal = row >= col

    for h in range(N_HEADS):
        lanes = slice(h * HEAD_DIM, (h + 1) * HEAD_DIM)
        qh = q_ref[0, :, lanes]
        ccol = c_ref[0, :, h:h + 1]

        def block(j, carry, masked, lanes=lanes, qh=qh, ccol=ccol, h=h):
            m, l, acc = carry
            start = pl.multiple_of(j * TK, TK)
            kh = k_ref[0, pl.ds(start, TK), lanes]
            vh = v_ref[0, pl.ds(start, TK), lanes]
            s = lax.dot_general(qh, kh, (((1,), (1,)), ((), ())), preferred_element_type=F32)
            s = s + (ccol - ct_ref[0, h, pl.ds(j, 1), :])
            if masked:
                s = jnp.where(causal, s, NEG_BIG)
            m_new = jnp.maximum(m, jnp.max(s, axis=-1, keepdims=True))
            p = jnp.exp(s - m_new)
            alpha = jnp.exp(m - m_new)
            l = alpha * l + jnp.sum(p, axis=-1, keepdims=True)
            acc = alpha * acc + _dot(p.astype(BF16), vh)
            return m_new, l, acc

        init = (jnp.full((TQ, 1), NEG_BIG, F32), jnp.zeros((TQ, 1), F32),
                jnp.zeros((TQ, HEAD_DIM), F32))
        carry = block(i, init, True)
        m, l, acc = lax.fori_loop(0, i, functools.partial(block, masked=False), carry)
        o_ref[0, :, lanes] = (acc / l).astype(BF16)


def _fox(q, k, v, c, ct):
    bsz, seq, _ = q.shape
    return pl.pallas_call(
        _fox_kernel,
        grid=(bsz, seq // TQ),
        in_specs=[
            pl.BlockSpec((1, TQ, D_HALF), lambda b, i: (b, i, 0)),
            pl.BlockSpec((1, seq, D_HALF), lambda b, i: (b, 0, 0)),
            pl.BlockSpec((1, seq, D_HALF), lambda b, i: (b, 0, 0)),
            pl.BlockSpec((1, TQ, LANES), lambda b, i: (b, i, 0)),
            pl.BlockSpec((1, N_HEADS, seq // TK, TK), lambda b, i: (b, 0, 0, 0)),
        ],
        out_specs=pl.BlockSpec((1, TQ, D_HALF), lambda b, i: (b, i, 0)),
        out_shape=jax.ShapeDtypeStruct((bsz, seq, D_HALF), BF16),
        compiler_params=pltpu.CompilerParams(
            dimension_semantics=("parallel", "arbitrary"),
            vmem_limit_bytes=VMEM_LIMIT),
        name="fox_attention",
    )(q, k, v, c, ct)


def _even_out_kernel(h_ref, ya_ref, yb_ref, wo_ref, gmp_ref, gfp_ref, wg_ref, wu_ref, wd_ref,
                     gfq_ref, out_ref):
    m = _dot(ya_ref[0], wo_ref[0:D_HALF, :]) + _dot(yb_ref[0], wo_ref[D_HALF:, :])
    h1 = h_ref[0] + _rms(m, gmp_ref[...])
    out_ref[0] = _ffn(h1, gfp_ref[...], wg_ref, wu_ref, wd_ref, gfq_ref[...])


def _ffn_specs():
    return [
        _const_spec((1, D_MODEL)),
        _const_spec((N_FF_CHUNKS, D_MODEL, FF_CHUNK)),
        _const_spec((N_FF_CHUNKS, D_MODEL, FF_CHUNK)),
        _const_spec((N_FF_CHUNKS, FF_CHUNK, D_MODEL)),
        _const_spec((1, D_MODEL)),
    ]


def _even_out(h, ya, yb, w_out, g_mix_post, ffn):
    bsz, seq, _ = h.shape
    return pl.pallas_call(
        _even_out_kernel,
        grid=(bsz, seq // TM),
        in_specs=[
            _tile_spec(TM, D_MODEL),
            _tile_spec(TM, D_HALF),
            _tile_spec(TM, D_HALF),
            _const_spec((D_MODEL, D_MODEL)),
            _const_spec((1, D_MODEL)),
        ] + _ffn_specs(),
        out_specs=_tile_spec(TM, D_MODEL),
        out_shape=jax.ShapeDtypeStruct(h.shape, F32),
        compiler_params=pltpu.CompilerParams(
            dimension_semantics=("parallel", "parallel"),
            vmem_limit_bytes=VMEM_LIMIT),
        name="even_out_ffn",
    )(h, ya, yb, w_out, g_mix_post, *ffn)


def _odd_kernel(h_ref, gpre_ref, w1_ref, b1_ref, wdw_ref, bdw_ref, lng_ref, lnb_ref, w2_ref, b2_ref,
                gmp_ref, gfp_ref, wg_ref, wu_ref, wd_ref, gfq_ref, out_ref, ubuf, cbuf):
    t = pl.program_id(1)

    @pl.when(t == 0)
    def _():
        ubuf[0:CONF_HALO, :] = jnp.zeros((CONF_HALO, D_MODEL), F32)

    h = h_ref[0]
    x = _rms(h, gpre_ref[...]).astype(BF16)
    a = _dot(x, w1_ref[:, 0:D_MODEL]) + b1_ref[:, 0:D_MODEL]
    g = _dot(x, w1_ref[:, D_MODEL:]) + b1_ref[:, D_MODEL:]
    ubuf[CONF_HALO:CONF_HALO + TM, :] = a * _sigmoid(g)

    base = CONF_HALO - (CONF_K - 1)
    for r in range(0, TM, CONV_ROWS):
        for c in range(0, D_MODEL, CONV_LANES):
            acc = jnp.broadcast_to(bdw_ref[:, c:c + CONV_LANES], (CONV_ROWS, CONV_LANES))
            for k in range(CONF_K):
                acc = acc + wdw_ref[k:k + 1, c:c + CONV_LANES] * \
                    ubuf[r + base + k:r + base + k + CONV_ROWS, c:c + CONV_LANES]
            cbuf[r:r + CONV_ROWS, c:c + CONV_LANES] = acc
    ubuf[0:CONF_HALO, :] = ubuf[TM:TM + CONF_HALO, :]

    y = cbuf[...]
    mu = jnp.mean(y, axis=-1, keepdims=True)
    yc = y - mu
    var = jnp.mean(yc * yc, axis=-1, keepdims=True)
    yn = yc * lax.rsqrt(var + LN_EPS) * lng_ref[...] + lnb_ref[...]
    sw = (yn * _sigmoid(yn)).astype(BF16)
    m = _dot(sw, w2_ref[...]) + b2_ref[...]
    h1 = h + _rms(m, gmp_ref[...])
    out_ref[0] = _ffn(h1, gfp_ref[...], wg_ref, wu_ref, wd_ref, gfq_ref[...])


def _odd_layer(h, g_pre, w1, b1, wdw, bdw, lng, lnb, w2, b2, g_mix_post, ffn):
    bsz, seq, _ = h.shape
    return pl.pallas_call(
        _odd_kernel,
        grid=(bsz, seq // TM),
        in_specs=[
            _tile_spec(TM, D_MODEL),
            _const_spec((1, D_MODEL)),
            _const_spec((D_MODEL, 2 * D_MODEL)),
            _const_spec((1, 2 * D_MODEL)),
            _const_spec((CONF_K, D_MODEL)),
            _const_spec((1, D_MODEL)),
            _const_spec((1, D_MODEL)),
            _const_spec((1, D_MODEL)),
            _const_spec((D_MODEL, D_MODEL)),
            _const_spec((1, D_MODEL)),
            _const_spec((1, D_MODEL)),
        ] + _ffn_specs(),
        out_specs=_tile_spec(TM, D_MODEL),
        out_shape=jax.ShapeDtypeStruct(h.shape, F32),
        scratch_shapes=[
            pltpu.VMEM((TM + CONF_HALO, D_MODEL), F32),
            pltpu.VMEM((TM, D_MODEL), F32),
        ],
        compiler_params=pltpu.CompilerParams(
            dimension_semantics=("parallel", "arbitrary"),
            vmem_limit_bytes=VMEM_LIMIT),
        name="odd_conformer_ffn",
    )(h, g_pre, w1, b1, wdw, bdw, lng, lnb, w2, b2, g_mix_post, *ffn)


def _row(v):
    return v.reshape(1, -1)


def kernel(x, norm_mix_pre, norm_mix_post, norm_ffn_pre, norm_ffn_post, w_in, b_forget, w_short_conv, w_out, w_pw1, b_pw1, w_dw, b_dw, ln_g, ln_b, w_pw2, b_pw2, w_ffn_in, w_ffn_out):
    bsz, seq, d_model = x.shape
    assert d_model == D_MODEL and seq % TM == 0 and seq % TQ == 0 and TQ == TK
    depth = norm_mix_pre.shape[0]
    tri = (lax.broadcasted_iota(jnp.int32, (TM, TM), 0)
           >= lax.broadcasted_iota(jnp.int32, (TM, TM), 1)).astype(BF16)
    n_main = 6 * D_HALF

    h = x
    for layer in range(depth):
        j = layer // 2
        wi = w_ffn_in[layer].astype(BF16)
        ffn = (
            _row(norm_ffn_pre[layer]),
            wi[:, :D_FF].reshape(D_MODEL, N_FF_CHUNKS, FF_CHUNK).transpose(1, 0, 2),
            wi[:, D_FF:].reshape(D_MODEL, N_FF_CHUNKS, FF_CHUNK).transpose(1, 0, 2),
            w_ffn_out[layer].astype(BF16).reshape(N_FF_CHUNKS, FF_CHUNK, D_MODEL),
            _row(norm_ffn_post[layer]),
        )
        if layer % 2 == 0:
            w_f = jnp.pad(w_in[j][:, n_main:], ((0, 0), (0, LANES - N_HEADS))).astype(BF16)
            b_f = jnp.pad(b_forget[j], (0, LANES - N_HEADS)).reshape(1, LANES)
            ya, q, k, v, c = _even_in(h, _row(norm_mix_pre[layer]), w_in[j][:, :n_main].astype(BF16),
                                      w_f, b_f, w_short_conv[j], tri)
            ct = c[:, :, :N_HEADS].transpose(0, 2, 1).reshape(bsz, N_HEADS, seq // TK, TK)
            yb = _fox(q, k, v, c, ct)
            h = _even_out(h, ya, yb, w_out[j].astype(BF16), _row(norm_mix_post[layer]), ffn)
        else:
            h = _odd_layer(h, _row(norm_mix_pre[layer]), w_pw1[j].astype(BF16), _row(b_pw1[j]),
                           w_dw[j], _row(b_dw[j]), _row(ln_g[j]), _row(ln_b[j]),
                           w_pw2[j].astype(BF16), _row(b_pw2[j]), _row(norm_mix_post[layer]), ffn)
    return h
```

```python
import numpy as np

import jax
import jax.numpy as jnp
from jax import lax
from jax.experimental import pallas as pl
from jax.experimental.pallas import tpu as pltpu

D_MODEL = 1024
HEAD_DIM = 64
N_HEADS = 8
D_HALF = D_MODEL // 2
SHORT_K = 3
CONF_K = 31
D_FF = 2816
RMS_EPS = 1e-6
LN_EPS = 1e-5
LANES = 128
SUBLANES = 8
BF16_ROWS = 16

TM = 512
TA = 256
FF_CHUNK = 256
N_FF_CHUNKS = D_FF // FF_CHUNK
EX_W = 16
V_ROWS = HEAD_DIM + BF16_ROWS
CONF_HALO = 32
CONV_SLAB = 256
CONV_PITCH = 2
CONV_ROWS = 64
VMEM_LIMIT = 56 * 1024 * 1024
NEG_BIG = -1e30

F32 = jnp.float32
BF16 = jnp.bfloat16


def _dot(a, b):
    return jnp.dot(a, b, preferred_element_type=F32)


def _rms(x, g):
    return x * lax.rsqrt(jnp.mean(x * x, axis=-1, keepdims=True) + RMS_EPS) * g


def _sigmoid(x):
    return 1.0 / (1.0 + jnp.exp(-x))


def _split3(x):
    hi = x.astype(BF16)
    r1 = x - hi.astype(F32)
    mid = r1.astype(BF16)
    lo = (r1 - mid.astype(F32)).astype(BF16)
    return hi, mid, lo


def _const_spec(shape):
    nd = len(shape)
    return pl.BlockSpec(shape, lambda b, t: (0,) * nd, pipeline_mode=pl.Buffered(1))


def _tile_spec(rows, cols):
    return pl.BlockSpec((1, rows, cols), lambda b, t: (b, t, 0))


def _ffn(h1, g_pre, wi_ref, wo_ref, g_post):
    x = _rms(h1, g_pre).astype(BF16)
    acc = jnp.zeros(h1.shape, F32)
    for c in range(N_FF_CHUNKS):
        lo = c * FF_CHUNK
        g = _dot(x, wi_ref[:, lo:lo + FF_CHUNK])
        u = _dot(x, wi_ref[:, D_FF + lo:D_FF + lo + FF_CHUNK])
        a = (g * _sigmoid(g) * u).astype(BF16)
        acc = acc + _dot(a, wo_ref[lo:lo + FF_CHUNK, :])
    return h1 + _rms(acc, g_post)


def _even_in_kernel(h_ref, g_ref, w_ref, wf_ref, bf_ref, cw_ref, tri_ref, pk_ref, pq_ref, ck_ref, cq_ref,
                    ya_ref, qt_ref, k_ref, vt_ref, ek_ref, eqt_ref, pbuf, ccarry):
    t = pl.program_id(1)

    @pl.when(t == 0)
    def _():
        pbuf[0:SUBLANES, :] = jnp.zeros((SUBLANES, D_HALF), F32)
        ccarry[...] = jnp.zeros(ccarry.shape, F32)

    x = _rms(h_ref[0], g_ref[...]).astype(BF16)

    def proj(i):
        return _dot(x, w_ref[:, i * D_HALF:(i + 1) * D_HALF])

    p = proj(1) * proj(2)
    pbuf[SUBLANES:SUBLANES + TM, :] = p
    p1 = pbuf[SUBLANES - 1:SUBLANES - 1 + TM, :]
    p2 = pbuf[SUBLANES - 2:SUBLANES - 2 + TM, :]
    conv = cw_ref[0:1, :] * p2 + cw_ref[1:2, :] * p1 + cw_ref[2:3, :] * p
    ya_ref[0] = (proj(0) * conv).astype(BF16)
    pbuf[0:SUBLANES, :] = pbuf[TM:TM + SUBLANES, :]

    qt = (proj(3) * (HEAD_DIM ** -0.5)).T.astype(BF16)
    k_ref[0] = proj(4).astype(BF16)
    vt = proj(5).T
    fill = (lax.broadcasted_iota(jnp.int32, (BF16_ROWS, TM), 0) == 0).astype(F32)
    vt = jnp.concatenate(
        [piece for h in range(N_HEADS) for piece in (vt[h * HEAD_DIM:(h + 1) * HEAD_DIM], fill)],
        axis=0).astype(BF16)

    z = _dot(x, wf_ref[...]) + bf_ref[...]
    lf = jnp.minimum(z, 0.0) - jnp.log1p(jnp.exp(-jnp.abs(z)))
    tri = tri_ref[...]
    hi, mid, lo = _split3(lf)
    c = (_dot(tri, hi) + _dot(tri, mid)) + _dot(tri, lo) + ccarry[0:1, :]
    ccarry[...] = jnp.broadcast_to(c[TM - 1:TM, :], ccarry.shape)

    hi, mid, lo = _split3(c)
    ek = (_dot(hi, pk_ref[0]) + _dot(mid, pk_ref[1])) + _dot(lo, pk_ref[2]) + ck_ref[...]
    eq = (_dot(hi, pq_ref[0]) + _dot(mid, pq_ref[1])) + _dot(lo, pq_ref[2]) + cq_ref[...]
    ek_ref[0] = ek.astype(BF16)
    eqt = eq.T.astype(BF16)

    for s in range(TM // TA):
        cols = slice(s * TA, (s + 1) * TA)
        qt_ref[0, s] = qt[:, cols]
        vt_ref[0, s] = vt[:, cols]
        eqt_ref[0, s] = eqt[:, cols]


def _even_in(h, g_pre, w_main, w_f, b_f, conv_w, consts):
    bsz, seq, _ = h.shape
    n_blk = seq // TA
    sub = TM // TA

    def blk_spec(rows):
        return pl.BlockSpec((1, sub, rows, TA), lambda b, t: (b, t, 0, 0))

    return pl.pallas_call(
        _even_in_kernel,
        grid=(bsz, seq // TM),
        in_specs=[
            _tile_spec(TM, D_MODEL),
            _const_spec((1, D_MODEL)),
            _const_spec((D_MODEL, 6 * D_HALF)),
            _const_spec((D_MODEL, LANES)),
            _const_spec((1, LANES)),
            _const_spec((SHORT_K, D_HALF)),
            _const_spec((TM, TM)),
            _const_spec((3, LANES, LANES)),
            _const_spec((3, LANES, LANES)),
            _const_spec((1, LANES)),
            _const_spec((1, LANES)),
        ],
        out_specs=[
            _tile_spec(TM, D_HALF),
            blk_spec(D_HALF),
            _tile_spec(TM, D_HALF),
            blk_spec(N_HEADS * V_ROWS),
            _tile_spec(TM, LANES),
            blk_spec(LANES),
        ],
        out_shape=[
            jax.ShapeDtypeStruct((bsz, seq, D_HALF), BF16),
            jax.ShapeDtypeStruct((bsz, n_blk, D_HALF, TA), BF16),
            jax.ShapeDtypeStruct((bsz, seq, D_HALF), BF16),
            jax.ShapeDtypeStruct((bsz, n_blk, N_HEADS * V_ROWS, TA), BF16),
            jax.ShapeDtypeStruct((bsz, seq, LANES), BF16),
            jax.ShapeDtypeStruct((bsz, n_blk, LANES, TA), BF16),
        ],
        scratch_shapes=[
            pltpu.VMEM((TM + 2 * SUBLANES, D_HALF), F32),
            pltpu.VMEM((SUBLANES, LANES), F32),
        ],
        compiler_params=pltpu.CompilerParams(
            dimension_semantics=("parallel", "arbitrary"),
            vmem_limit_bytes=VMEM_LIMIT),
        name="even_in_proj",
    )(h, g_pre, w_main, w_f, b_f, conv_w, *consts)


def _bias_placement():
    pk = np.zeros((3, LANES, LANES), np.float32)
    pq = np.zeros((3, LANES, LANES), np.float32)
    ck = np.zeros((1, LANES), np.float32)
    cq = np.zeros((1, LANES), np.float32)
    for h in range(N_HEADS):
        for i in range(3):
            pk[i, h, EX_W * h + i] = -1.0
            cq[0, EX_W * h + i] = 1.0
            pq[i, h, EX_W * h + 3 + i] = 1.0
            ck[0, EX_W * h + 3 + i] = 1.0
    tri = np.tril(np.ones((TM, TM), np.float32))
    return (jnp.asarray(tri, BF16), jnp.asarray(pk, BF16), jnp.asarray(pq, BF16),
            jnp.asarray(ck), jnp.asarray(cq))


def _fox_kernel(qt_ref, eqt_ref, k_ref, ek_ref, vt_ref, o_ref, qs, acc, m_sc, mx_sc, al_sc, s_sc, p_sc):
    i = pl.program_id(1)

    zeros_q = jnp.zeros((HEAD_DIM, TA), BF16)
    for h in range(N_HEADS):
        qh = qt_ref[0, 0, h * HEAD_DIM:(h + 1) * HEAD_DIM, :]
        pieces = [qh, zeros_q] if h % 2 == 0 else [zeros_q, qh]
        if h > 0:
            pieces.append(jnp.zeros((EX_W * h, TA), BF16))
        pieces.append(eqt_ref[0, 0, EX_W * h:EX_W * (h + 1), :])
        if h < N_HEADS - 1:
            pieces.append(jnp.zeros((LANES - EX_W * (h + 1), TA), BF16))
        qs[h] = jnp.concatenate(pieces, axis=0)
        acc[h] = jnp.zeros((V_ROWS, TA), F32)
        m_sc[h] = jnp.full((SUBLANES, TA), NEG_BIG, F32)

    key_idx = lax.broadcasted_iota(jnp.int32, (TA, TA), 0)
    qry_idx = lax.broadcasted_iota(jnp.int32, (TA, TA), 1)
    causal = key_idx <= qry_idx

    def step(j, masked):
        r0 = pl.multiple_of(j * TA, TA)
        ekb = ek_ref[0, pl.ds(r0, TA), :]
        for pair in range(N_HEADS // 2):
            kk = jnp.concatenate([k_ref[0, pl.ds(r0, TA), pair * LANES:(pair + 1) * LANES], ekb], axis=1)
            for h in (2 * pair, 2 * pair + 1):
                st = _dot(kk, qs[h])
                if masked:
                    st = jnp.where(causal, st, NEG_BIG)
                s_sc[h] = st
                mx_sc[h, 0:1, :] = jnp.max(st, axis=0, keepdims=True)
        for h in range(N_HEADS):
            m_old = m_sc[h, 0:1, :]
            m_new = jnp.maximum(m_old, mx_sc[h, 0:1, :])
            al_sc[h, 0:1, :] = jnp.exp(m_old - m_new)
            m_sc[h, 0:1, :] = m_new
            p_sc[h] = jnp.exp(s_sc[h] - m_new).astype(BF16)
        for h in range(N_HEADS):
            acc[h] = al_sc[h, 0:1, :] * acc[h] + _dot(vt_ref[0, j, h * V_ROWS:(h + 1) * V_ROWS, :], p_sc[h])

    step(i, True)

    def body(j, carry):
        step(j, False)
        return carry

    lax.fori_loop(0, i, body, 0)

    outs = []
    for h in range(N_HEADS):
        a = acc[h]
        outs.append(a[0:HEAD_DIM] / a[HEAD_DIM:HEAD_DIM + 1])
    o_ref[0] = jnp.concatenate(outs, axis=0).T.astype(BF16)


def _fox(qt, eqt, k, ek, vt):
    bsz, seq, _ = k.shape
    n_blk = seq // TA
    return pl.pallas_call(
        _fox_kernel,
        grid=(bsz, n_blk),
        in_specs=[
            pl.BlockSpec((1, 1, D_HALF, TA), lambda b, i: (b, i, 0, 0)),
            pl.BlockSpec((1, 1, LANES, TA), lambda b, i: (b, i, 0, 0)),
            pl.BlockSpec((1, seq, D_HALF), lambda b, i: (b, 0, 0)),
            pl.BlockSpec((1, seq, LANES), lambda b, i: (b, 0, 0)),
            pl.BlockSpec((1, n_blk, N_HEADS * V_ROWS, TA), lambda b, i: (b, 0, 0, 0)),
        ],
        out_specs=pl.BlockSpec((1, TA, D_HALF), lambda b, i: (b, i, 0)),
        out_shape=jax.ShapeDtypeStruct((bsz, seq, D_HALF), BF16),
        scratch_shapes=[
            pltpu.VMEM((N_HEADS, 2 * LANES, TA), BF16),
            pltpu.VMEM((N_HEADS, V_ROWS, TA), F32),
            pltpu.VMEM((N_HEADS, SUBLANES, TA), F32),
            pltpu.VMEM((N_HEADS, SUBLANES, TA), F32),
            pltpu.VMEM((N_HEADS, SUBLANES, TA), F32),
            pltpu.VMEM((N_HEADS, TA, TA), F32),
            pltpu.VMEM((N_HEADS, TA, TA), BF16),
        ],
        compiler_params=pltpu.CompilerParams(
            dimension_semantics=("parallel", "arbitrary"),
            vmem_limit_bytes=VMEM_LIMIT),
        name="fox_attention",
    )(qt, eqt, k, ek, vt)


def _even_out_kernel(h_ref, ya_ref, yb_ref, wo_ref, gmp_ref, gfp_ref, wi_ref, wd_ref, gfq_ref, out_ref):
    m = _dot(ya_ref[0], wo_ref[0:D_HALF, :]) + _dot(yb_ref[0], wo_ref[D_HALF:, :])
    h1 = h_ref[0] + _rms(m, gmp_ref[...])
    out_ref[0] = _ffn(h1, gfp_ref[...], wi_ref, wd_ref, gfq_ref[...])


def _ffn_specs():
    return [
        _const_spec((1, D_MODEL)),
        _const_spec((D_MODEL, 2 * D_FF)),
        _const_spec((D_FF, D_MODEL)),
        _const_spec((1, D_MODEL)),
    ]


def _even_out(h, ya, yb, w_out, g_mix_post, ffn):
    bsz, seq, _ = h.shape
    return pl.pallas_call(
        _even_out_kernel,
        grid=(bsz, seq // TM),
        in_specs=[
            _tile_spec(TM, D_MODEL),
            _tile_spec(TM, D_HALF),
            _tile_spec(TM, D_HALF),
            _const_spec((D_MODEL, D_MODEL)),
            _const_spec((1, D_MODEL)),
        ] + _ffn_specs(),
        out_specs=_tile_spec(TM, D_MODEL),
        out_shape=jax.ShapeDtypeStruct(h.shape, F32),
        compiler_params=pltpu.CompilerParams(
            dimension_semantics=("parallel", "parallel"),
            vmem_limit_bytes=VMEM_LIMIT),
        name="even_out_ffn",
    )(h, ya, yb, w_out, g_mix_post, *ffn)


def _odd_kernel(h_ref, gpre_ref, w1_ref, b1_ref, wdw_ref, bdw_ref, lng_ref, lnb_ref, w2_ref, b2_ref,
                gmp_ref, gfp_ref, wi_ref, wd_ref, gfq_ref, out_ref, ubuf, cbuf):
    t = pl.program_id(1)
    halo = CONV_PITCH * CONF_HALO

    @pl.when(t == 0)
    def _():
        ubuf[:, 0:halo, :] = jnp.zeros((D_MODEL // LANES, halo, LANES), F32)

    h = h_ref[0]
    x = _rms(h, gpre_ref[...]).astype(BF16)

    base = CONF_HALO - (CONF_K - 1)
    for s2 in range(D_MODEL // CONV_SLAB):
        lanes2 = slice(s2 * CONV_SLAB, (s2 + 1) * CONV_SLAB)
        glanes2 = slice(D_MODEL + s2 * CONV_SLAB, D_MODEL + (s2 + 1) * CONV_SLAB)
        a = _dot(x, w1_ref[:, lanes2]) + b1_ref[:, lanes2]
        g = _dot(x, w1_ref[:, glanes2]) + b1_ref[:, glanes2]
        u = a * _sigmoid(g)
        for half in range(CONV_SLAB // LANES):
            s = s2 * (CONV_SLAB // LANES) + half
            lanes = slice(s * LANES, (s + 1) * LANES)
            ubuf[s, pl.ds(halo, TM, stride=CONV_PITCH), :] = u[:, half * LANES:(half + 1) * LANES]
            for r in range(0, TM, CONV_ROWS):
                y = jnp.broadcast_to(bdw_ref[:, lanes], (CONV_ROWS, LANES))
                for k in range(CONF_K):
                    win = ubuf[s, pl.ds(CONV_PITCH * (r + base + k), CONV_ROWS, stride=CONV_PITCH), :]
                    y = y + wdw_ref[k:k + 1, lanes] * win
                cbuf[r:r + CONV_ROWS, lanes] = y
            ubuf[s, pl.ds(0, CONF_HALO, stride=CONV_PITCH), :] = \
                ubuf[s, pl.ds(CONV_PITCH * TM, CONF_HALO, stride=CONV_PITCH), :]

    y = cbuf[...]
    mu = jnp.mean(y, axis=-1, keepdims=True)
    yc = y - mu
    var = jnp.mean(yc * yc, axis=-1, keepdims=True)
    yn = yc * lax.rsqrt(var + LN_EPS) * lng_ref[...] + lnb_ref[...]
    sw = (yn * _sigmoid(yn)).astype(BF16)
    m = _dot(sw, w2_ref[...]) + b2_ref[...]
    h1 = h + _rms(m, gmp_ref[...])
    out_ref[0] = _ffn(h1, gfp_ref[...], wi_ref, wd_ref, gfq_ref[...])


def _odd_layer(h, g_pre, w1, b1, wdw, bdw, lng, lnb, w2, b2, g_mix_post, ffn):
    bsz, seq, _ = h.shape
    return pl.pallas_call(
        _odd_kernel,
        grid=(bsz, seq // TM),
        in_specs=[
            _tile_spec(TM, D_MODEL),
            _const_spec((1, D_MODEL)),
            _const_spec((D_MODEL, 2 * D_MODEL)),
            _const_spec((1, 2 * D_MODEL)),
            _const_spec((CONF_K, D_MODEL)),
            _const_spec((1, D_MODEL)),
            _const_spec((1, D_MODEL)),
            _const_spec((1, D_MODEL)),
            _const_spec((D_MODEL, D_MODEL)),
            _const_spec((1, D_MODEL)),
            _const_spec((1, D_MODEL)),
        ] + _ffn_specs(),
        out_specs=_tile_spec(TM, D_MODEL),
        out_shape=jax.ShapeDtypeStruct(h.shape, F32),
        scratch_shapes=[
            pltpu.VMEM((D_MODEL // LANES, CONV_PITCH * (TM + CONF_HALO), LANES), F32),
            pltpu.VMEM((TM, D_MODEL), F32),
        ],
        compiler_params=pltpu.CompilerParams(
            dimension_semantics=("parallel", "arbitrary"),
            vmem_limit_bytes=VMEM_LIMIT),
        name="odd_conformer_ffn",
    )(h, g_pre, w1, b1, wdw, bdw, lng, lnb, w2, b2, g_mix_post, *ffn)


def _row(v):
    return v.reshape(1, -1)


def kernel(x, norm_mix_pre, norm_mix_post, norm_ffn_pre, norm_ffn_post, w_in, b_forget, w_short_conv, w_out, w_pw1, b_pw1, w_dw, b_dw, ln_g, ln_b, w_pw2, b_pw2, w_ffn_in, w_ffn_out):
    bsz, seq, d_model = x.shape
    assert d_model == D_MODEL and seq % TM == 0 and TM % TA == 0
    depth = norm_mix_pre.shape[0]
    consts = _bias_placement()
    n_main = 6 * D_HALF

    h = x
    for layer in range(depth):
        j = layer // 2
        ffn = (_row(norm_ffn_pre[layer]), w_ffn_in[layer].astype(BF16), w_ffn_out[layer].astype(BF16),
               _row(norm_ffn_post[layer]))
        if layer % 2 == 0:
            w_f = jnp.pad(w_in[j][:, n_main:], ((0, 0), (0, LANES - N_HEADS))).astype(BF16)
            b_f = jnp.pad(b_forget[j], (0, LANES - N_HEADS)).reshape(1, LANES)
            ya, qt, k, vt, ek, eqt = _even_in(h, _row(norm_mix_pre[layer]), w_in[j][:, :n_main].astype(BF16),
                                              w_f, b_f, w_short_conv[j], consts)
            yb = _fox(qt, eqt, k, ek, vt)
            h = _even_out(h, ya, yb, w_out[j].astype(BF16), _row(norm_mix_post[layer]), ffn)
        else:
            h = _odd_layer(h, _row(norm_mix_pre[layer]), w_pw1[j].astype(BF16), _row(b_pw1[j]),
                           w_dw[j], _row(b_dw[j]), _row(ln_g[j]), _row(ln_b[j]),
                           w_pw2[j].astype(BF16), _row(b_pw2[j]), _row(norm_mix_post[layer]), ffn)
    return h
```

```python
import numpy as np

import jax
import jax.numpy as jnp
from jax import lax
from jax.experimental import pallas as pl
from jax.experimental.pallas import tpu as pltpu

D_MODEL = 1024
HEAD_DIM = 64
N_HEADS = 8
D_HALF = D_MODEL // 2
SHORT_K = 3
CONF_K = 31
D_FF = 2816
RMS_EPS = 1e-6
LN_EPS = 1e-5
LANES = 128
SUBLANES = 8
BF16_ROWS = 16

TM = 512
TA = 256
FF_CHUNK = 256
N_FF_CHUNKS = D_FF // FF_CHUNK
EX_W = 16
V_ROWS = HEAD_DIM + BF16_ROWS
CONF_HALO = 32
CONV_SLAB = 256
CONV_PITCH = 2
CONV_ROWS = 64
VMEM_LIMIT = 56 * 1024 * 1024
NEG_BIG = -1e30

F32 = jnp.float32
BF16 = jnp.bfloat16


def _dot(a, b):
    return jnp.dot(a, b, preferred_element_type=F32)


def _rms(x, g):
    return x * lax.rsqrt(jnp.mean(x * x, axis=-1, keepdims=True) + RMS_EPS) * g


def _sigmoid(x):
    return 1.0 / (1.0 + jnp.exp(-x))


def _split3(x):
    hi = x.astype(BF16)
    r1 = x - hi.astype(F32)
    mid = r1.astype(BF16)
    lo = (r1 - mid.astype(F32)).astype(BF16)
    return hi, mid, lo


def _const_spec(shape):
    nd = len(shape)
    return pl.BlockSpec(shape, lambda *_: (0,) * nd, pipeline_mode=pl.Buffered(1))


def _tile_spec(rows, cols):
    return pl.BlockSpec((1, rows, cols), lambda b, t: (b, t, 0))


def _ffn(h1, g_pre, wi_ref, wo_ref, g_post):
    x = _rms(h1, g_pre).astype(BF16)
    acc = jnp.zeros(h1.shape, F32)
    for c in range(N_FF_CHUNKS):
        lo = c * FF_CHUNK
        g = _dot(x, wi_ref[:, lo:lo + FF_CHUNK])
        u = _dot(x, wi_ref[:, D_FF + lo:D_FF + lo + FF_CHUNK])
        a = (g * _sigmoid(g) * u).astype(BF16)
        acc = acc + _dot(a, wo_ref[lo:lo + FF_CHUNK, :])
    return h1 + _rms(acc, g_post)


def _even_in_kernel(h_ref, g_ref, w_ref, wf_ref, bf_ref, cw_ref, tri_ref, pkq_ref, ck_ref, cq_ref,
                    ya_ref, qt_ref, k_ref, vt_ref, ek_ref, eqt_ref, pbuf, ccarry):
    t = pl.program_id(1)

    @pl.when(t == 0)
    def _():
        pbuf[0:SUBLANES, :] = jnp.zeros((SUBLANES, D_HALF), F32)
        ccarry[...] = jnp.zeros(ccarry.shape, F32)

    x = _rms(h_ref[0], g_ref[...]).astype(BF16)

    def proj(i):
        return _dot(x, w_ref[:, i * D_HALF:(i + 1) * D_HALF])

    p = proj(1) * proj(2)
    pbuf[SUBLANES:SUBLANES + TM, :] = p
    p1 = pbuf[SUBLANES - 1:SUBLANES - 1 + TM, :]
    p2 = pbuf[SUBLANES - 2:SUBLANES - 2 + TM, :]
    conv = cw_ref[0:1, :] * p2 + cw_ref[1:2, :] * p1 + cw_ref[2:3, :] * p
    ya_ref[0] = (proj(0) * conv).astype(BF16)
    pbuf[0:SUBLANES, :] = pbuf[TM:TM + SUBLANES, :]

    qt = (proj(3) * (HEAD_DIM ** -0.5)).T.astype(BF16)
    k_ref[0] = proj(4).astype(BF16)
    vt = proj(5).T
    fill = (lax.broadcasted_iota(jnp.int32, (BF16_ROWS, TM), 0) == 0).astype(F32)
    vt = jnp.concatenate(
        [piece for h in range(N_HEADS) for piece in (vt[h * HEAD_DIM:(h + 1) * HEAD_DIM], fill)],
        axis=0).astype(BF16)

    z = _dot(x, wf_ref[...]) + bf_ref[...]
    lf = jnp.minimum(z, 0.0) - jnp.log1p(jnp.exp(-jnp.abs(z)))
    tri = tri_ref[...]
    hi, mid, lo = _split3(lf)
    c2 = _dot(tri, jnp.concatenate([hi, mid], axis=1))
    c = (c2[:, :LANES] + c2[:, LANES:]) + _dot(tri, lo) + ccarry[0:1, :]
    ccarry[...] = jnp.broadcast_to(c[TM - 1:TM, :], ccarry.shape)

    hi, mid, lo = _split3(c)
    e2 = (_dot(hi, pkq_ref[0]) + _dot(mid, pkq_ref[1])) + _dot(lo, pkq_ref[2])
    ek_ref[0] = (e2[:, :LANES] + ck_ref[...]).astype(BF16)
    eqt = (e2[:, LANES:] + cq_ref[...]).T.astype(BF16)

    for s in range(TM // TA):
        cols = slice(s * TA, (s + 1) * TA)
        qt_ref[0, s] = qt[:, cols]
        vt_ref[0, s] = vt[:, cols]
        eqt_ref[0, s] = eqt[:, cols]


def _even_in(h, g_pre, w_main, w_f, b_f, conv_w, consts):
    bsz, seq, _ = h.shape
    n_blk = seq // TA
    sub = TM // TA

    def blk_spec(rows):
        return pl.BlockSpec((1, sub, rows, TA), lambda b, t: (b, t, 0, 0))

    return pl.pallas_call(
        _even_in_kernel,
        grid=(bsz, seq // TM),
        in_specs=[
            _tile_spec(TM, D_MODEL),
            _const_spec((1, D_MODEL)),
            _const_spec((D_MODEL, 6 * D_HALF)),
            _const_spec((D_MODEL, LANES)),
            _const_spec((1, LANES)),
            _const_spec((SHORT_K, D_HALF)),
            _const_spec((TM, TM)),
            _const_spec((3, LANES, 2 * LANES)),
            _const_spec((1, LANES)),
            _const_spec((1, LANES)),
        ],
        out_specs=[
            _tile_spec(TM, D_HALF),
            blk_spec(D_HALF),
            _tile_spec(TM, D_HALF),
            blk_spec(N_HEADS * V_ROWS),
            _tile_spec(TM, LANES),
            blk_spec(LANES),
        ],
        out_shape=[
            jax.ShapeDtypeStruct((bsz, seq, D_HALF), BF16),
            jax.ShapeDtypeStruct((bsz, n_blk, D_HALF, TA), BF16),
            jax.ShapeDtypeStruct((bsz, seq, D_HALF), BF16),
            jax.ShapeDtypeStruct((bsz, n_blk, N_HEADS * V_ROWS, TA), BF16),
            jax.ShapeDtypeStruct((bsz, seq, LANES), BF16),
            jax.ShapeDtypeStruct((bsz, n_blk, LANES, TA), BF16),
        ],
        scratch_shapes=[
            pltpu.VMEM((TM + 2 * SUBLANES, D_HALF), F32),
            pltpu.VMEM((SUBLANES, LANES), F32),
        ],
        compiler_params=pltpu.CompilerParams(
            dimension_semantics=("parallel", "arbitrary"),
            vmem_limit_bytes=VMEM_LIMIT),
        name="even_in_proj",
    )(h, g_pre, w_main, w_f, b_f, conv_w, *consts)


def _bias_placement():
    pk = np.zeros((3, LANES, LANES), np.float32)
    pq = np.zeros((3, LANES, LANES), np.float32)
    ck = np.zeros((1, LANES), np.float32)
    cq = np.zeros((1, LANES), np.float32)
    for h in range(N_HEADS):
        for i in range(3):
            pk[i, h, EX_W * h + i] = -1.0
            cq[0, EX_W * h + i] = 1.0
            pq[i, h, EX_W * h + 3 + i] = 1.0
            ck[0, EX_W * h + 3 + i] = 1.0
    tri = np.tril(np.ones((TM, TM), np.float32))
    return (jnp.asarray(tri, BF16), jnp.asarray(np.concatenate([pk, pq], axis=2), BF16),
            jnp.asarray(ck), jnp.asarray(cq))


def _fox_kernel(qt_ref, eqt_ref, k_ref, ek_ref, vt_ref, o_ref, qs, acc, m_sc, mx_sc, al_sc, s_sc, p_sc):
    i = pl.program_id(1)

    zeros_q = jnp.zeros((HEAD_DIM, TA), BF16)
    for h in range(N_HEADS):
        qh = qt_ref[0, 0, h * HEAD_DIM:(h + 1) * HEAD_DIM, :]
        pieces = [qh, zeros_q] if h % 2 == 0 else [zeros_q, qh]
        if h > 0:
            pieces.append(jnp.zeros((EX_W * h, TA), BF16))
        pieces.append(eqt_ref[0, 0, EX_W * h:EX_W * (h + 1), :])
        if h < N_HEADS - 1:
            pieces.append(jnp.zeros((LANES - EX_W * (h + 1), TA), BF16))
        qs[h] = jnp.concatenate(pieces, axis=0)
        acc[h] = jnp.zeros((V_ROWS, TA), F32)
        m_sc[h] = jnp.full((SUBLANES, TA), NEG_BIG, F32)

    key_idx = lax.broadcasted_iota(jnp.int32, (TA, TA), 0)
    qry_idx = lax.broadcasted_iota(jnp.int32, (TA, TA), 1)
    causal = key_idx <= qry_idx

    def scores(h, kk, masked):
        st = _dot(kk, qs[h])
        if masked:
            st = jnp.where(causal, st, NEG_BIG)
        s_sc[h] = st
        mx_sc[h, 0:1, :] = jnp.max(st, axis=0, keepdims=True)

    def key_pairs(j):
        r0 = pl.multiple_of(j * TA, TA)
        ekb = ek_ref[0, pl.ds(r0, TA), :]
        return [jnp.concatenate([k_ref[0, pl.ds(r0, TA), p * LANES:(p + 1) * LANES], ekb], axis=1)
                for p in range(N_HEADS // 2)]

    def probs(h):
        m_old = m_sc[h, 0:1, :]
        m_new = jnp.maximum(m_old, mx_sc[h, 0:1, :])
        al_sc[h, 0:1, :] = jnp.exp(m_old - m_new)
        m_sc[h, 0:1, :] = m_new
        p_sc[h] = jnp.exp(s_sc[h] - m_new).astype(BF16)

    def accumulate(j, h):
        acc[h] = al_sc[h, 0:1, :] * acc[h] + _dot(vt_ref[0, j, h * V_ROWS:(h + 1) * V_ROWS, :], p_sc[h])

    kks = key_pairs(i)
    for h in range(N_HEADS):
        scores(h, kks[h // 2], True)

    def body(j, carry):
        prev = jnp.where(j == 0, i, j - 1)
        kks = key_pairs(j)
        for h in range(N_HEADS):
            probs(h)
            scores(h, kks[h // 2], False)
        for h in range(N_HEADS):
            accumulate(prev, h)
        return carry

    lax.fori_loop(0, i, body, 0)
    last = jnp.maximum(i - 1, 0)
    for h in range(N_HEADS):
        probs(h)
    for h in range(N_HEADS):
        accumulate(last, h)

    outs = []
    for h in range(N_HEADS):
        a = acc[h]
        outs.append(a[0:HEAD_DIM] / a[HEAD_DIM:HEAD_DIM + 1])
    o_ref[0] = jnp.concatenate(outs, axis=0).T.astype(BF16)


def _fox(qt, eqt, k, ek, vt):
    bsz, seq, _ = k.shape
    n_blk = seq // TA
    return pl.pallas_call(
        _fox_kernel,
        grid=(bsz, n_blk),
        in_specs=[
            pl.BlockSpec((1, 1, D_HALF, TA), lambda b, i: (b, i, 0, 0)),
            pl.BlockSpec((1, 1, LANES, TA), lambda b, i: (b, i, 0, 0)),
            pl.BlockSpec((1, seq, D_HALF), lambda b, i: (b, 0, 0)),
            pl.BlockSpec((1, seq, LANES), lambda b, i: (b, 0, 0)),
            pl.BlockSpec((1, n_blk, N_HEADS * V_ROWS, TA), lambda b, i: (b, 0, 0, 0)),
        ],
        out_specs=pl.BlockSpec((1, TA, D_HALF), lambda b, i: (b, i, 0)),
        out_shape=jax.ShapeDtypeStruct((bsz, seq, D_HALF), BF16),
        scratch_shapes=[
            pltpu.VMEM((N_HEADS, 2 * LANES, TA), BF16),
            pltpu.VMEM((N_HEADS, V_ROWS, TA), F32),
            pltpu.VMEM((N_HEADS, SUBLANES, TA), F32),
            pltpu.VMEM((N_HEADS, SUBLANES, TA), F32),
            pltpu.VMEM((N_HEADS, SUBLANES, TA), F32),
            pltpu.VMEM((N_HEADS, TA, TA), F32),
            pltpu.VMEM((N_HEADS, TA, TA), BF16),
        ],
        compiler_params=pltpu.CompilerParams(
            dimension_semantics=("parallel", "arbitrary"),
            vmem_limit_bytes=VMEM_LIMIT),
        name="fox_attention",
    )(qt, eqt, k, ek, vt)


def _even_out_kernel(h_ref, ya_ref, yb_ref, wo_ref, gmp_ref, gfp_ref, wi_ref, wd_ref, gfq_ref, out_ref):
    m = _dot(ya_ref[0], wo_ref[0:D_HALF, :]) + _dot(yb_ref[0], wo_ref[D_HALF:, :])
    h1 = h_ref[0] + _rms(m, gmp_ref[...])
    out_ref[0] = _ffn(h1, gfp_ref[...], wi_ref, wd_ref, gfq_ref[...])


def _ffn_specs():
    return [
        _const_spec((1, D_MODEL)),
        _const_spec((D_MODEL, 2 * D_FF)),
        _const_spec((D_FF, D_MODEL)),
        _const_spec((1, D_MODEL)),
    ]


def _even_out(h, ya, yb, w_out, g_mix_post, ffn):
    bsz, seq, _ = h.shape
    return pl.pallas_call(
        _even_out_kernel,
        grid=(bsz, seq // TM),
        in_specs=[
            _tile_spec(TM, D_MODEL),
            _tile_spec(TM, D_HALF),
            _tile_spec(TM, D_HALF),
            _const_spec((D_MODEL, D_MODEL)),
            _const_spec((1, D_MODEL)),
        ] + _ffn_specs(),
        out_specs=_tile_spec(TM, D_MODEL),
        out_shape=jax.ShapeDtypeStruct(h.shape, F32),
        compiler_params=pltpu.CompilerParams(
            dimension_semantics=("parallel", "parallel"),
            vmem_limit_bytes=VMEM_LIMIT),
        name="even_out_ffn",
    )(h, ya, yb, w_out, g_mix_post, *ffn)


def _odd_kernel(h_ref, gpre_ref, w1_ref, b1_ref, wdw_ref, bdw_ref, lng_ref, lnb_ref, w2_ref, b2_ref,
                gmp_ref, gfp_ref, wi_ref, wd_ref, gfq_ref, out_ref, ubuf, cbuf):
    t = pl.program_id(1)
    halo = CONV_PITCH * CONF_HALO

    @pl.when(t == 0)
    def _():
        ubuf[:, 0:halo, :] = jnp.zeros((D_MODEL // LANES, halo, LANES), F32)

    h = h_ref[0]
    x = _rms(h, gpre_ref[...]).astype(BF16)

    base = CONF_HALO - (CONF_K - 1)
    for s2 in range(D_MODEL // CONV_SLAB):
        lanes2 = slice(s2 * CONV_SLAB, (s2 + 1) * CONV_SLAB)
        glanes2 = slice(D_MODEL + s2 * CONV_SLAB, D_MODEL + (s2 + 1) * CONV_SLAB)
        a = _dot(x, w1_ref[:, lanes2]) + b1_ref[:, lanes2]
        g = _dot(x, w1_ref[:, glanes2]) + b1_ref[:, glanes2]
        u = a * _sigmoid(g)
        for half in range(CONV_SLAB // LANES):
            s = s2 * (CONV_SLAB // LANES) + half
            lanes = slice(s * LANES, (s + 1) * LANES)
            ubuf[s, pl.ds(halo, TM, stride=CONV_PITCH), :] = u[:, half * LANES:(half + 1) * LANES]
            for r in range(0, TM, CONV_ROWS):
                y = jnp.broadcast_to(bdw_ref[:, lanes], (CONV_ROWS, LANES))
                for k in range(CONF_K):
                    win = ubuf[s, pl.ds(CONV_PITCH * (r + base + k), CONV_ROWS, stride=CONV_PITCH), :]
                    y = y + wdw_ref[k:k + 1, lanes] * win
                cbuf[r:r + CONV_ROWS, lanes] = y
            ubuf[s, pl.ds(0, CONF_HALO, stride=CONV_PITCH), :] = \
                ubuf[s, pl.ds(CONV_PITCH * TM, CONF_HALO, stride=CONV_PITCH), :]

    y = cbuf[...]
    mu = jnp.mean(y, axis=-1, keepdims=True)
    yc = y - mu
    var = jnp.mean(yc * yc, axis=-1, keepdims=True)
    yn = yc * lax.rsqrt(var + LN_EPS) * lng_ref[...] + lnb_ref[...]
    sw = (yn * _sigmoid(yn)).astype(BF16)
    m = _dot(sw, w2_ref[...]) + b2_ref[...]
    h1 = h + _rms(m, gmp_ref[...])
    out_ref[0] = _ffn(h1, gfp_ref[...], wi_ref, wd_ref, gfq_ref[...])


def _odd_layer(h, g_pre, w1, b1, wdw, bdw, lng, lnb, w2, b2, g_mix_post, ffn):
    bsz, seq, _ = h.shape
    return pl.pallas_call(
        _odd_kernel,
        grid=(bsz, seq // TM),
        in_specs=[
            _tile_spec(TM, D_MODEL),
            _const_spec((1, D_MODEL)),
            _const_spec((D_MODEL, 2 * D_MODEL)),
            _const_spec((1, 2 * D_MODEL)),
            _const_spec((CONF_K, D_MODEL)),
            _const_spec((1, D_MODEL)),
            _const_spec((1, D_MODEL)),
            _const_spec((1, D_MODEL)),
            _const_spec((D_MODEL, D_MODEL)),
            _const_spec((1, D_MODEL)),
            _const_spec((1, D_MODEL)),
        ] + _ffn_specs(),
        out_specs=_tile_spec(TM, D_MODEL),
        out_shape=jax.ShapeDtypeStruct(h.shape, F32),
        scratch_shapes=[
            pltpu.VMEM((D_MODEL // LANES, CONV_PITCH * (TM + CONF_HALO), LANES), F32),
            pltpu.VMEM((TM, D_MODEL), F32),
        ],
        compiler_params=pltpu.CompilerParams(
            dimension_semantics=("parallel", "arbitrary"),
            vmem_limit_bytes=VMEM_LIMIT),
        name="odd_conformer_ffn",
    )(h, g_pre, w1, b1, wdw, bdw, lng, lnb, w2, b2, g_mix_post, *ffn)


def _row(v):
    return v.reshape(1, -1)


def kernel(x, norm_mix_pre, norm_mix_post, norm_ffn_pre, norm_ffn_post, w_in, b_forget, w_short_conv, w_out, w_pw1, b_pw1, w_dw, b_dw, ln_g, ln_b, w_pw2, b_pw2, w_ffn_in, w_ffn_out):
    bsz, seq, d_model = x.shape
    assert d_model == D_MODEL and seq % TM == 0 and TM % TA == 0
    depth = norm_mix_pre.shape[0]
    consts = _bias_placement()
    n_main = 6 * D_HALF

    h = x
    for layer in range(depth):
        j = layer // 2
        ffn = (_row(norm_ffn_pre[layer]), w_ffn_in[layer].astype(BF16), w_ffn_out[layer].astype(BF16),
               _row(norm_ffn_post[layer]))
        if layer % 2 == 0:
            w_f = jnp.pad(w_in[j][:, n_main:], ((0, 0), (0, LANES - N_HEADS))).astype(BF16)
            b_f = jnp.pad(b_forget[j], (0, LANES - N_HEADS)).reshape(1, LANES)
            ya, qt, k, vt, ek, eqt = _even_in(h, _row(norm_mix_pre[layer]), w_in[j][:, :n_main].astype(BF16),
                                              w_f, b_f, w_short_conv[j], consts)
            yb = _fox(qt, eqt, k, ek, vt)
            h = _even_out(h, ya, yb, w_out[j].astype(BF16), _row(norm_mix_post[layer]), ffn)
        else:
            h = _odd_layer(h, _row(norm_mix_pre[layer]), w_pw1[j].astype(BF16), _row(b_pw1[j]),
                           w_dw[j], _row(b_dw[j]), _row(ln_g[j]), _row(ln_b[j]),
                           w_pw2[j].astype(BF16), _row(b_pw2[j]), _row(norm_mix_post[layer]), ffn)
    return h
```

```python
import numpy as np

import jax
import jax.numpy as jnp
from jax import lax
from jax.experimental import pallas as pl
from jax.experimental.pallas import tpu as pltpu

D_MODEL = 1024
HEAD_DIM = 64
N_HEADS = 8
D_HALF = D_MODEL // 2
SHORT_K = 3
CONF_K = 31
D_FF = 2816
RMS_EPS = 1e-6
LN_EPS = 1e-5
LANES = 128
SUBLANES = 8
BF16_ROWS = 16

TM = 512
TA = 256
FF_CHUNK = 256
N_FF_CHUNKS = D_FF // FF_CHUNK
EX_W = 16
V_ROWS = HEAD_DIM + BF16_ROWS
CONF_HALO = 32
CONV_SLAB = 256
CONV_PITCH = 2
CONV_ROWS = 64
VMEM_LIMIT = 56 * 1024 * 1024
NEG_BIG = -1e30

F32 = jnp.float32
BF16 = jnp.bfloat16


def _dot(a, b):
    return jnp.dot(a, b, preferred_element_type=F32)


def _rms(x, g):
    return x * lax.rsqrt(jnp.mean(x * x, axis=-1, keepdims=True) + RMS_EPS) * g


def _sigmoid(x):
    return 1.0 / (1.0 + jnp.exp(-x))


def _split3(x):
    hi = x.astype(BF16)
    r1 = x - hi.astype(F32)
    mid = r1.astype(BF16)
    lo = (r1 - mid.astype(F32)).astype(BF16)
    return hi, mid, lo


def _const_spec(shape):
    nd = len(shape)
    return pl.BlockSpec(shape, lambda *_: (0,) * nd, pipeline_mode=pl.Buffered(1))


def _tile_spec(rows, cols):
    return pl.BlockSpec((1, rows, cols), lambda b, t: (b, t, 0))


def _ffn(h1, g_pre, wi_ref, wo_ref, g_post):
    x = _rms(h1, g_pre).astype(BF16)
    acc = jnp.zeros(h1.shape, F32)
    for c in range(N_FF_CHUNKS):
        lo = c * FF_CHUNK
        g = _dot(x, wi_ref[:, lo:lo + FF_CHUNK])
        u = _dot(x, wi_ref[:, D_FF + lo:D_FF + lo + FF_CHUNK])
        a = (g * _sigmoid(g) * u).astype(BF16)
        acc = acc + _dot(a, wo_ref[lo:lo + FF_CHUNK, :])
    return h1 + _rms(acc, g_post)


def _even_in_kernel(h_ref, g_ref, w_ref, wf_ref, bf_ref, cw_ref, tri_ref, pkq_ref, ck_ref, cq_ref,
                    ya_ref, qt_ref, k_ref, vt_ref, ek_ref, eqt_ref, pbuf, ccarry):
    t = pl.program_id(1)

    @pl.when(t == 0)
    def _():
        pbuf[0:SUBLANES, :] = jnp.zeros((SUBLANES, D_HALF), F32)
        ccarry[...] = jnp.zeros(ccarry.shape, F32)

    x = _rms(h_ref[0], g_ref[...]).astype(BF16)

    def proj(i):
        return _dot(x, w_ref[:, i * D_HALF:(i + 1) * D_HALF])

    p = proj(1) * proj(2)
    pbuf[SUBLANES:SUBLANES + TM, :] = p
    p1 = pbuf[SUBLANES - 1:SUBLANES - 1 + TM, :]
    p2 = pbuf[SUBLANES - 2:SUBLANES - 2 + TM, :]
    conv = cw_ref[0:1, :] * p2 + cw_ref[1:2, :] * p1 + cw_ref[2:3, :] * p
    ya_ref[0] = (proj(0) * conv).astype(BF16)
    pbuf[0:SUBLANES, :] = pbuf[TM:TM + SUBLANES, :]

    qt = (proj(3) * (HEAD_DIM ** -0.5)).T.astype(BF16)
    k_ref[0] = proj(4).astype(BF16)
    vt = proj(5).T
    fill = (lax.broadcasted_iota(jnp.int32, (BF16_ROWS, TM), 0) == 0).astype(F32)
    vt = jnp.concatenate(
        [piece for h in range(N_HEADS) for piece in (vt[h * HEAD_DIM:(h + 1) * HEAD_DIM], fill)],
        axis=0).astype(BF16)

    z = _dot(x, wf_ref[...]) + bf_ref[...]
    lf = jnp.minimum(z, 0.0) - jnp.log1p(jnp.exp(-jnp.abs(z)))
    tri = tri_ref[...]
    hi, mid, lo = _split3(lf)
    c2 = _dot(tri, jnp.concatenate([hi, mid], axis=1))
    c = (c2[:, :LANES] + c2[:, LANES:]) + _dot(tri, lo) + ccarry[0:1, :]
    ccarry[...] = jnp.broadcast_to(c[TM - 1:TM, :], ccarry.shape)

    hi, mid, lo = _split3(c)
    e2 = (_dot(hi, pkq_ref[0]) + _dot(mid, pkq_ref[1])) + _dot(lo, pkq_ref[2])
    ek_ref[0] = (e2[:, :LANES] + ck_ref[...]).astype(BF16)
    eqt = (e2[:, LANES:] + cq_ref[...]).T.astype(BF16)

    for s in range(TM // TA):
        cols = slice(s * TA, (s + 1) * TA)
        qt_ref[0, s] = qt[:, cols]
        vt_ref[0, s] = vt[:, cols]
        eqt_ref[0, s] = eqt[:, cols]


def _even_in(h, g_pre, w_main, w_f, b_f, conv_w, consts):
    bsz, seq, _ = h.shape
    n_blk = seq // TA
    sub = TM // TA

    def blk_spec(rows):
        return pl.BlockSpec((1, sub, rows, TA), lambda b, t: (b, t, 0, 0))

    return pl.pallas_call(
        _even_in_kernel,
        grid=(bsz, seq // TM),
        in_specs=[
            _tile_spec(TM, D_MODEL),
            _const_spec((1, D_MODEL)),
            _const_spec((D_MODEL, 6 * D_HALF)),
            _const_spec((D_MODEL, LANES)),
            _const_spec((1, LANES)),
            _const_spec((SHORT_K, D_HALF)),
            _const_spec((TM, TM)),
            _const_spec((3, LANES, 2 * LANES)),
            _const_spec((1, LANES)),
            _const_spec((1, LANES)),
        ],
        out_specs=[
            _tile_spec(TM, D_HALF),
            blk_spec(D_HALF),
            _tile_spec(TM, D_HALF),
            blk_spec(N_HEADS * V_ROWS),
            _tile_spec(TM, LANES),
            blk_spec(LANES),
        ],
        out_shape=[
            jax.ShapeDtypeStruct((bsz, seq, D_HALF), BF16),
            jax.ShapeDtypeStruct((bsz, n_blk, D_HALF, TA), BF16),
            jax.ShapeDtypeStruct((bsz, seq, D_HALF), BF16),
            jax.ShapeDtypeStruct((bsz, n_blk, N_HEADS * V_ROWS, TA), BF16),
            jax.ShapeDtypeStruct((bsz, seq, LANES), BF16),
            jax.ShapeDtypeStruct((bsz, n_blk, LANES, TA), BF16),
        ],
        scratch_shapes=[
            pltpu.VMEM((TM + 2 * SUBLANES, D_HALF), F32),
            pltpu.VMEM((SUBLANES, LANES), F32),
        ],
        compiler_params=pltpu.CompilerParams(
            dimension_semantics=("parallel", "arbitrary"),
            vmem_limit_bytes=VMEM_LIMIT),
        name="even_in_proj",
    )(h, g_pre, w_main, w_f, b_f, conv_w, *consts)


def _bias_placement():
    pk = np.zeros((3, LANES, LANES), np.float32)
    pq = np.zeros((3, LANES, LANES), np.float32)
    ck = np.zeros((1, LANES), np.float32)
    cq = np.zeros((1, LANES), np.float32)
    for h in range(N_HEADS):
        for i in range(3):
            pk[i, h, EX_W * h + i] = -1.0
            cq[0, EX_W * h + i] = 1.0
            pq[i, h, EX_W * h + 3 + i] = 1.0
            ck[0, EX_W * h + 3 + i] = 1.0
    tri = np.tril(np.ones((TM, TM), np.float32))
    return (jnp.asarray(tri, BF16), jnp.asarray(np.concatenate([pk, pq], axis=2), BF16),
            jnp.asarray(ck), jnp.asarray(cq))


def _fox_kernel(qt_ref, eqt_ref, k_ref, ek_ref, vt_ref, o_ref, qs, acc, m_sc, mx_sc, al_sc, s_sc, p_sc):
    n_blk = qt_ref.shape[1]

    def query_block(i, carry):
        _fox_query_block(i, qt_ref, eqt_ref, k_ref, ek_ref, vt_ref, o_ref, qs, acc, m_sc, mx_sc, al_sc, s_sc, p_sc)
        return carry

    lax.fori_loop(0, n_blk, query_block, 0)


def _fox_query_block(i, qt_ref, eqt_ref, k_ref, ek_ref, vt_ref, o_ref, qs, acc, m_sc, mx_sc, al_sc, s_sc, p_sc):
    zeros_q = jnp.zeros((HEAD_DIM, TA), BF16)
    for h in range(N_HEADS):
        qh = qt_ref[0, i, h * HEAD_DIM:(h + 1) * HEAD_DIM, :]
        pieces = [qh, zeros_q] if h % 2 == 0 else [zeros_q, qh]
        if h > 0:
            pieces.append(jnp.zeros((EX_W * h, TA), BF16))
        pieces.append(eqt_ref[0, i, EX_W * h:EX_W * (h + 1), :])
        if h < N_HEADS - 1:
            pieces.append(jnp.zeros((LANES - EX_W * (h + 1), TA), BF16))
        qs[h] = jnp.concatenate(pieces, axis=0)
        acc[h] = jnp.zeros((V_ROWS, TA), F32)
        m_sc[h] = jnp.full((SUBLANES, TA), NEG_BIG, F32)

    key_idx = lax.broadcasted_iota(jnp.int32, (TA, TA), 0)
    qry_idx = lax.broadcasted_iota(jnp.int32, (TA, TA), 1)
    causal = key_idx <= qry_idx

    def scores(h, kk, masked):
        st = _dot(kk, qs[h])
        if masked:
            st = jnp.where(causal, st, NEG_BIG)
        s_sc[h] = st
        mx_sc[h, 0:1, :] = jnp.max(st, axis=0, keepdims=True)

    def key_pairs(j):
        r0 = pl.multiple_of(j * TA, TA)
        ekb = ek_ref[0, pl.ds(r0, TA), :]
        return [jnp.concatenate([k_ref[0, pl.ds(r0, TA), p * LANES:(p + 1) * LANES], ekb], axis=1)
                for p in range(N_HEADS // 2)]

    def probs(h):
        m_old = m_sc[h, 0:1, :]
        m_new = jnp.maximum(m_old, mx_sc[h, 0:1, :])
        al_sc[h, 0:1, :] = jnp.exp(m_old - m_new)
        m_sc[h, 0:1, :] = m_new
        p_sc[h] = jnp.exp(s_sc[h] - m_new).astype(BF16)

    def accumulate(j, h):
        acc[h] = al_sc[h, 0:1, :] * acc[h] + _dot(vt_ref[0, j, h * V_ROWS:(h + 1) * V_ROWS, :], p_sc[h])

    kks = key_pairs(i)
    for h in range(N_HEADS):
        scores(h, kks[h // 2], True)

    def body(j, carry):
        prev = jnp.where(j == 0, i, j - 1)
        kks = key_pairs(j)
        for h in range(N_HEADS):
            probs(h)
            scores(h, kks[h // 2], False)
        for h in range(N_HEADS):
            accumulate(prev, h)
        return carry

    lax.fori_loop(0, i, body, 0)
    last = jnp.maximum(i - 1, 0)
    for h in range(N_HEADS):
        probs(h)
    for h in range(N_HEADS):
        accumulate(last, h)

    outs = []
    for h in range(N_HEADS):
        a = acc[h]
        outs.append(a[0:HEAD_DIM] / a[HEAD_DIM:HEAD_DIM + 1])
    o_ref[0, pl.ds(pl.multiple_of(i * TA, TA), TA), :] = jnp.concatenate(outs, axis=0).T.astype(BF16)


def _fox(qt, eqt, k, ek, vt):
    bsz, seq, _ = k.shape
    n_blk = seq // TA
    return pl.pallas_call(
        _fox_kernel,
        grid=(bsz,),
        in_specs=[
            pl.BlockSpec((1, n_blk, D_HALF, TA), lambda b: (b, 0, 0, 0)),
            pl.BlockSpec((1, n_blk, LANES, TA), lambda b: (b, 0, 0, 0)),
            pl.BlockSpec((1, seq, D_HALF), lambda b: (b, 0, 0)),
            pl.BlockSpec((1, seq, LANES), lambda b: (b, 0, 0)),
            pl.BlockSpec((1, n_blk, N_HEADS * V_ROWS, TA), lambda b: (b, 0, 0, 0)),
        ],
        out_specs=pl.BlockSpec((1, seq, D_HALF), lambda b: (b, 0, 0)),
        out_shape=jax.ShapeDtypeStruct((bsz, seq, D_HALF), BF16),
        scratch_shapes=[
            pltpu.VMEM((N_HEADS, 2 * LANES, TA), BF16),
            pltpu.VMEM((N_HEADS, V_ROWS, TA), F32),
            pltpu.VMEM((N_HEADS, SUBLANES, TA), F32),
            pltpu.VMEM((N_HEADS, SUBLANES, TA), F32),
            pltpu.VMEM((N_HEADS, SUBLANES, TA), F32),
            pltpu.VMEM((N_HEADS, TA, TA), F32),
            pltpu.VMEM((N_HEADS, TA, TA), BF16),
        ],
        compiler_params=pltpu.CompilerParams(
            dimension_semantics=("parallel",),
            vmem_limit_bytes=VMEM_LIMIT),
        name="fox_attention",
    )(qt, eqt, k, ek, vt)


def _even_out_kernel(h_ref, ya_ref, yb_ref, wo_ref, gmp_ref, gfp_ref, wi_ref, wd_ref, gfq_ref, out_ref):
    m = _dot(ya_ref[0], wo_ref[0:D_HALF, :]) + _dot(yb_ref[0], wo_ref[D_HALF:, :])
    h1 = h_ref[0] + _rms(m, gmp_ref[...])
    out_ref[0] = _ffn(h1, gfp_ref[...], wi_ref, wd_ref, gfq_ref[...])


def _ffn_specs():
    return [
        _const_spec((1, D_MODEL)),
        _const_spec((D_MODEL, 2 * D_FF)),
        _const_spec((D_FF, D_MODEL)),
        _const_spec((1, D_MODEL)),
    ]


def _even_out(h, ya, yb, w_out, g_mix_post, ffn):
    bsz, seq, _ = h.shape
    return pl.pallas_call(
        _even_out_kernel,
        grid=(bsz, seq // TM),
        in_specs=[
            _tile_spec(TM, D_MODEL),
            _tile_spec(TM, D_HALF),
            _tile_spec(TM, D_HALF),
            _const_spec((D_MODEL, D_MODEL)),
            _const_spec((1, D_MODEL)),
        ] + _ffn_specs(),
        out_specs=_tile_spec(TM, D_MODEL),
        out_shape=jax.ShapeDtypeStruct(h.shape, F32),
        compiler_params=pltpu.CompilerParams(
            dimension_semantics=("parallel", "parallel"),
            vmem_limit_bytes=VMEM_LIMIT),
        name="even_out_ffn",
    )(h, ya, yb, w_out, g_mix_post, *ffn)


def _odd_kernel(h_ref, gpre_ref, w1_ref, b1_ref, wdw_ref, bdw_ref, lng_ref, lnb_ref, w2_ref, b2_ref,
                gmp_ref, gfp_ref, wi_ref, wd_ref, gfq_ref, out_ref, ubuf, cbuf):
    t = pl.program_id(1)
    halo = CONV_PITCH * CONF_HALO

    @pl.when(t == 0)
    def _():
        ubuf[:, 0:halo, :] = jnp.zeros((D_MODEL // LANES, halo, LANES), F32)

    h = h_ref[0]
    x = _rms(h, gpre_ref[...]).astype(BF16)

    base = CONF_HALO - (CONF_K - 1)
    for s2 in range(D_MODEL // CONV_SLAB):
        lanes2 = slice(s2 * CONV_SLAB, (s2 + 1) * CONV_SLAB)
        glanes2 = slice(D_MODEL + s2 * CONV_SLAB, D_MODEL + (s2 + 1) * CONV_SLAB)
        a = _dot(x, w1_ref[:, lanes2]) + b1_ref[:, lanes2]
        g = _dot(x, w1_ref[:, glanes2]) + b1_ref[:, glanes2]
        u = a * _sigmoid(g)
        for half in range(CONV_SLAB // LANES):
            s = s2 * (CONV_SLAB // LANES) + half
            lanes = slice(s * LANES, (s + 1) * LANES)
            ubuf[s, pl.ds(halo, TM, stride=CONV_PITCH), :] = u[:, half * LANES:(half + 1) * LANES]
            for r in range(0, TM, CONV_ROWS):
                y = jnp.broadcast_to(bdw_ref[:, lanes], (CONV_ROWS, LANES))
                for k in range(CONF_K):
                    win = ubuf[s, pl.ds(CONV_PITCH * (r + base + k), CONV_ROWS, stride=CONV_PITCH), :]
                    y = y + wdw_ref[k:k + 1, lanes] * win
                cbuf[r:r + CONV_ROWS, lanes] = y
            ubuf[s, pl.ds(0, CONF_HALO, stride=CONV_PITCH), :] = \
                ubuf[s, pl.ds(CONV_PITCH * TM, CONF_HALO, stride=CONV_PITCH), :]

    y = cbuf[...]
    mu = jnp.mean(y, axis=-1, keepdims=True)
    yc = y - mu
    var = jnp.mean(yc * yc, axis=-1, keepdims=True)
    yn = yc * lax.rsqrt(var + LN_EPS) * lng_ref[...] + lnb_ref[...]
    sw = (yn * _sigmoid(yn)).astype(BF16)
    m = _dot(sw, w2_ref[...]) + b2_ref[...]
    h1 = h + _rms(m, gmp_ref[...])
    out_ref[0] = _ffn(h1, gfp_ref[...], wi_ref, wd_ref, gfq_ref[...])


def _odd_layer(h, g_pre, w1, b1, wdw, bdw, lng, lnb, w2, b2, g_mix_post, ffn):
    bsz, seq, _ = h.shape
    return pl.pallas_call(
        _odd_kernel,
        grid=(bsz, seq // TM),
        in_specs=[
            _tile_spec(TM, D_MODEL),
            _const_spec((1, D_MODEL)),
            _const_spec((D_MODEL, 2 * D_MODEL)),
            _const_spec((1, 2 * D_MODEL)),
            _const_spec((CONF_K, D_MODEL)),
            _const_spec((1, D_MODEL)),
            _const_spec((1, D_MODEL)),
            _const_spec((1, D_MODEL)),
            _const_spec((D_MODEL, D_MODEL)),
            _const_spec((1, D_MODEL)),
            _const_spec((1, D_MODEL)),
        ] + _ffn_specs(),
        out_specs=_tile_spec(TM, D_MODEL),
        out_shape=jax.ShapeDtypeStruct(h.shape, F32),
        scratch_shapes=[
            pltpu.VMEM((D_MODEL // LANES, CONV_PITCH * (TM + CONF_HALO), LANES), F32),
            pltpu.VMEM((TM, D_MODEL), F32),
        ],
        compiler_params=pltpu.CompilerParams(
            dimension_semantics=("parallel", "arbitrary"),
            vmem_limit_bytes=VMEM_LIMIT),
        name="odd_conformer_ffn",
    )(h, g_pre, w1, b1, wdw, bdw, lng, lnb, w2, b2, g_mix_post, *ffn)


def _row(v):
    return v.reshape(1, -1)


def kernel(x, norm_mix_pre, norm_mix_post, norm_ffn_pre, norm_ffn_post, w_in, b_forget, w_short_conv, w_out, w_pw1, b_pw1, w_dw, b_dw, ln_g, ln_b, w_pw2, b_pw2, w_ffn_in, w_ffn_out):
    bsz, seq, d_model = x.shape
    assert d_model == D_MODEL and seq % TM == 0 and TM % TA == 0
    depth = norm_mix_pre.shape[0]
    consts = _bias_placement()
    n_main = 6 * D_HALF

    h = x
    for layer in range(depth):
        j = layer // 2
        ffn = (_row(norm_ffn_pre[layer]), w_ffn_in[layer].astype(BF16), w_ffn_out[layer].astype(BF16),
               _row(norm_ffn_post[layer]))
        if layer % 2 == 0:
            w_f = jnp.pad(w_in[j][:, n_main:], ((0, 0), (0, LANES - N_HEADS))).astype(BF16)
            b_f = jnp.pad(b_forget[j], (0, LANES - N_HEADS)).reshape(1, LANES)
            ya, qt, k, vt, ek, eqt = _even_in(h, _row(norm_mix_pre[layer]), w_in[j][:, :n_main].astype(BF16),
                                              w_f, b_f, w_short_conv[j], consts)
            yb = _fox(qt, eqt, k, ek, vt)
            h = _even_out(h, ya, yb, w_out[j].astype(BF16), _row(norm_mix_post[layer]), ffn)
        else:
            h = _odd_layer(h, _row(norm_mix_pre[layer]), w_pw1[j].astype(BF16), _row(b_pw1[j]),
                           w_dw[j], _row(b_dw[j]), _row(ln_g[j]), _row(ln_b[j]),
                           w_pw2[j].astype(BF16), _row(b_pw2[j]), _row(norm_mix_post[layer]), ffn)
    return h
```

```python
import numpy as np

import jax
import jax.numpy as jnp
from jax import lax
from jax.experimental import pallas as pl
from jax.experimental.pallas import tpu as pltpu

D_MODEL = 1024
HEAD_DIM = 64
N_HEADS = 8
D_HALF = D_MODEL // 2
SHORT_K = 3
CONF_K = 31
D_FF = 2816
RMS_EPS = 1e-6
LN_EPS = 1e-5
LANES = 128
SUBLANES = 8
BF16_ROWS = 16

TM = 512
TA = 256
FF_CHUNK = 256
N_FF_CHUNKS = D_FF // FF_CHUNK
EX_W = 16
V_ROWS = HEAD_DIM + BF16_ROWS
ACC_LAG = 2
CONF_HALO = 32
CONV_SLAB = 256
CONV_PITCH = 2
CONV_ROWS = 64
VMEM_LIMIT = 56 * 1024 * 1024
NEG_BIG = -1e30

F32 = jnp.float32
BF16 = jnp.bfloat16


def _dot(a, b):
    return jnp.dot(a, b, preferred_element_type=F32)


def _rms(x, g):
    return x * lax.rsqrt(jnp.mean(x * x, axis=-1, keepdims=True) + RMS_EPS) * g


def _sigmoid(x):
    return 1.0 / (1.0 + jnp.exp(-x))


def _split3(x):
    hi = x.astype(BF16)
    r1 = x - hi.astype(F32)
    mid = r1.astype(BF16)
    lo = (r1 - mid.astype(F32)).astype(BF16)
    return hi, mid, lo


def _const_spec(shape):
    nd = len(shape)
    return pl.BlockSpec(shape, lambda *_: (0,) * nd, pipeline_mode=pl.Buffered(1))


def _tile_spec(rows, cols):
    return pl.BlockSpec((1, rows, cols), lambda b, t: (b, t, 0))


def _ffn(h1, g_pre, wi_ref, wo_ref, g_post):
    x = _rms(h1, g_pre).astype(BF16)
    acc = jnp.zeros(h1.shape, F32)
    for c in range(N_FF_CHUNKS):
        lo = c * FF_CHUNK
        g = _dot(x, wi_ref[:, lo:lo + FF_CHUNK])
        u = _dot(x, wi_ref[:, D_FF + lo:D_FF + lo + FF_CHUNK])
        a = (g * _sigmoid(g) * u).astype(BF16)
        acc = acc + _dot(a, wo_ref[lo:lo + FF_CHUNK, :])
    return h1 + _rms(acc, g_post)


def _even_in_kernel(h_ref, g_ref, w_ref, wf_ref, bf_ref, cw_ref, tri_ref, pkq_ref, ck_ref, cq_ref,
                    ya_ref, qt_ref, k_ref, vt_ref, ek_ref, eqt_ref, pbuf, ccarry):
    t = pl.program_id(1)

    @pl.when(t == 0)
    def _():
        pbuf[0:SUBLANES, :] = jnp.zeros((SUBLANES, D_HALF), F32)
        ccarry[...] = jnp.zeros(ccarry.shape, F32)

    x = _rms(h_ref[0], g_ref[...]).astype(BF16)

    def proj(i):
        return _dot(x, w_ref[:, i * D_HALF:(i + 1) * D_HALF])

    z = _dot(x, wf_ref[...]) + bf_ref[...]
    lf = jnp.minimum(z, 0.0) - jnp.log1p(jnp.exp(-jnp.abs(z)))
    hi, mid, lo = _split3(lf)

    p = proj(1) * proj(2)
    pbuf[SUBLANES:SUBLANES + TM, :] = p
    p1 = pbuf[SUBLANES - 1:SUBLANES - 1 + TM, :]
    p2 = pbuf[SUBLANES - 2:SUBLANES - 2 + TM, :]
    conv = cw_ref[0:1, :] * p2 + cw_ref[1:2, :] * p1 + cw_ref[2:3, :] * p
    ya_ref[0] = (proj(0) * conv).astype(BF16)
    pbuf[0:SUBLANES, :] = pbuf[TM:TM + SUBLANES, :]

    tri = tri_ref[...]
    c2 = _dot(tri, jnp.concatenate([hi, mid], axis=1))
    c = (c2[:, :LANES] + c2[:, LANES:]) + _dot(tri, lo) + ccarry[0:1, :]
    ccarry[...] = jnp.broadcast_to(c[TM - 1:TM, :], ccarry.shape)
    hi, mid, lo = _split3(c)

    qt = (proj(3) * (HEAD_DIM ** -0.5)).T.astype(BF16)

    e2 = (_dot(hi, pkq_ref[0]) + _dot(mid, pkq_ref[1])) + _dot(lo, pkq_ref[2])
    ek_ref[0] = (e2[:, :LANES] + ck_ref[...]).astype(BF16)
    eqt = (e2[:, LANES:] + cq_ref[...]).T.astype(BF16)

    k_ref[0] = proj(4).astype(BF16)
    vt = proj(5).T
    fill = (lax.broadcasted_iota(jnp.int32, (BF16_ROWS, TM), 0) == 0).astype(F32)
    vt = jnp.concatenate(
        [piece for h in range(N_HEADS) for piece in (vt[h * HEAD_DIM:(h + 1) * HEAD_DIM], fill)],
        axis=0).astype(BF16)

    for s in range(TM // TA):
        cols = slice(s * TA, (s + 1) * TA)
        qt_ref[0, s] = qt[:, cols]
        vt_ref[0, s] = vt[:, cols]
        eqt_ref[0, s] = eqt[:, cols]


def _even_in(h, g_pre, w_main, w_f, b_f, conv_w, consts):
    bsz, seq, _ = h.shape
    n_blk = seq // TA
    sub = TM // TA

    def blk_spec(rows):
        return pl.BlockSpec((1, sub, rows, TA), lambda b, t: (b, t, 0, 0))

    return pl.pallas_call(
        _even_in_kernel,
        grid=(bsz, seq // TM),
        in_specs=[
            _tile_spec(TM, D_MODEL),
            _const_spec((1, D_MODEL)),
            _const_spec((D_MODEL, 6 * D_HALF)),
            _const_spec((D_MODEL, LANES)),
            _const_spec((1, LANES)),
            _const_spec((SHORT_K, D_HALF)),
            _const_spec((TM, TM)),
            _const_spec((3, LANES, 2 * LANES)),
            _const_spec((1, LANES)),
            _const_spec((1, LANES)),
        ],
        out_specs=[
            _tile_spec(TM, D_HALF),
            blk_spec(D_HALF),
            _tile_spec(TM, D_HALF),
            blk_spec(N_HEADS * V_ROWS),
            _tile_spec(TM, LANES),
            blk_spec(LANES),
        ],
        out_shape=[
            jax.ShapeDtypeStruct((bsz, seq, D_HALF), BF16),
            jax.ShapeDtypeStruct((bsz, n_blk, D_HALF, TA), BF16),
            jax.ShapeDtypeStruct((bsz, seq, D_HALF), BF16),
            jax.ShapeDtypeStruct((bsz, n_blk, N_HEADS * V_ROWS, TA), BF16),
            jax.ShapeDtypeStruct((bsz, seq, LANES), BF16),
            jax.ShapeDtypeStruct((bsz, n_blk, LANES, TA), BF16),
        ],
        scratch_shapes=[
            pltpu.VMEM((TM + 2 * SUBLANES, D_HALF), F32),
            pltpu.VMEM((SUBLANES, LANES), F32),
        ],
        compiler_params=pltpu.CompilerParams(
            dimension_semantics=("parallel", "arbitrary"),
            vmem_limit_bytes=VMEM_LIMIT),
        name="even_in_proj",
    )(h, g_pre, w_main, w_f, b_f, conv_w, *consts)


def _bias_placement():
    pk = np.zeros((3, LANES, LANES), np.float32)
    pq = np.zeros((3, LANES, LANES), np.float32)
    ck = np.zeros((1, LANES), np.float32)
    cq = np.zeros((1, LANES), np.float32)
    for h in range(N_HEADS):
        for i in range(3):
            pk[i, h, EX_W * h + i] = -1.0
            cq[0, EX_W * h + i] = 1.0
            pq[i, h, EX_W * h + 3 + i] = 1.0
            ck[0, EX_W * h + 3 + i] = 1.0
    tri = np.tril(np.ones((TM, TM), np.float32))
    return (jnp.asarray(tri, BF16), jnp.asarray(np.concatenate([pk, pq], axis=2), BF16),
            jnp.asarray(ck), jnp.asarray(cq))


def _fox_kernel(qt_ref, eqt_ref, k_ref, ek_ref, vt_ref, o_ref, qs, acc, m_sc, mx_sc, al_sc, s_sc, p_sc):
    n_blk = qt_ref.shape[1]

    def query_block(i, carry):
        _fox_query_block(i, qt_ref, eqt_ref, k_ref, ek_ref, vt_ref, o_ref, qs, acc, m_sc, mx_sc, al_sc, s_sc, p_sc)
        return carry

    lax.fori_loop(0, n_blk, query_block, 0)


def _fox_query_block(i, qt_ref, eqt_ref, k_ref, ek_ref, vt_ref, o_ref, qs, acc, m_sc, mx_sc, al_sc, s_sc, p_sc):
    zeros_q = jnp.zeros((HEAD_DIM, TA), BF16)
    for h in range(N_HEADS):
        qh = qt_ref[0, i, h * HEAD_DIM:(h + 1) * HEAD_DIM, :]
        pieces = [qh, zeros_q] if h % 2 == 0 else [zeros_q, qh]
        if h > 0:
            pieces.append(jnp.zeros((EX_W * h, TA), BF16))
        pieces.append(eqt_ref[0, i, EX_W * h:EX_W * (h + 1), :])
        if h < N_HEADS - 1:
            pieces.append(jnp.zeros((LANES - EX_W * (h + 1), TA), BF16))
        qs[h] = jnp.concatenate(pieces, axis=0)
        acc[h] = jnp.zeros((V_ROWS, TA), F32)
        m_sc[h] = jnp.full((SUBLANES, TA), NEG_BIG, F32)

    key_idx = lax.broadcasted_iota(jnp.int32, (TA, TA), 0)
    qry_idx = lax.broadcasted_iota(jnp.int32, (TA, TA), 1)
    causal = key_idx <= qry_idx

    def scores(h, kk, masked):
        st = _dot(kk, qs[h])
        if masked:
            st = jnp.where(causal, st, NEG_BIG)
        s_sc[h] = st
        mx_sc[h, 0:1, :] = jnp.max(st, axis=0, keepdims=True)

    def key_pairs(j):
        r0 = pl.multiple_of(j * TA, TA)
        ekb = ek_ref[0, pl.ds(r0, TA), :]
        return [jnp.concatenate([k_ref[0, pl.ds(r0, TA), p * LANES:(p + 1) * LANES], ekb], axis=1)
                for p in range(N_HEADS // 2)]

    def probs(h):
        m_old = m_sc[h, 0:1, :]
        m_new = jnp.maximum(m_old, mx_sc[h, 0:1, :])
        al_sc[h, 0:1, :] = jnp.exp(m_old - m_new)
        m_sc[h, 0:1, :] = m_new
        p_sc[h] = jnp.exp(s_sc[h] - m_new).astype(BF16)

    def accumulate(j, h):
        acc[h] = al_sc[h, 0:1, :] * acc[h] + _dot(vt_ref[0, j, h * V_ROWS:(h + 1) * V_ROWS, :], p_sc[h])

    kks = key_pairs(i)
    for h in range(N_HEADS):
        scores(h, kks[h // 2], True)

    def body(j, carry):
        prev = jnp.where(j == 0, i, j - 1)
        kks = key_pairs(j)
        for h in range(N_HEADS):
            probs(h)
            scores(h, kks[h // 2], False)
            if h >= ACC_LAG:
                accumulate(prev, h - ACC_LAG)
        for h in range(N_HEADS - ACC_LAG, N_HEADS):
            accumulate(prev, h)
        return carry

    lax.fori_loop(0, i, body, 0)
    last = jnp.maximum(i - 1, 0)
    for h in range(N_HEADS):
        probs(h)
        if h >= ACC_LAG:
            accumulate(last, h - ACC_LAG)
    for h in range(N_HEADS - ACC_LAG, N_HEADS):
        accumulate(last, h)

    outs = []
    for h in range(N_HEADS):
        a = acc[h]
        outs.append(a[0:HEAD_DIM] / a[HEAD_DIM:HEAD_DIM + 1])
    o_ref[0, pl.ds(pl.multiple_of(i * TA, TA), TA), :] = jnp.concatenate(outs, axis=0).T.astype(BF16)


def _fox(qt, eqt, k, ek, vt):
    bsz, seq, _ = k.shape
    n_blk = seq // TA
    return pl.pallas_call(
        _fox_kernel,
        grid=(bsz,),
        in_specs=[
            pl.BlockSpec((1, n_blk, D_HALF, TA), lambda b: (b, 0, 0, 0)),
            pl.BlockSpec((1, n_blk, LANES, TA), lambda b: (b, 0, 0, 0)),
            pl.BlockSpec((1, seq, D_HALF), lambda b: (b, 0, 0)),
            pl.BlockSpec((1, seq, LANES), lambda b: (b, 0, 0)),
            pl.BlockSpec((1, n_blk, N_HEADS * V_ROWS, TA), lambda b: (b, 0, 0, 0)),
        ],
        out_specs=pl.BlockSpec((1, seq, D_HALF), lambda b: (b, 0, 0)),
        out_shape=jax.ShapeDtypeStruct((bsz, seq, D_HALF), BF16),
        scratch_shapes=[
            pltpu.VMEM((N_HEADS, 2 * LANES, TA), BF16),
            pltpu.VMEM((N_HEADS, V_ROWS, TA), F32),
            pltpu.VMEM((N_HEADS, SUBLANES, TA), F32),
            pltpu.VMEM((N_HEADS, SUBLANES, TA), F32),
            pltpu.VMEM((N_HEADS, SUBLANES, TA), F32),
            pltpu.VMEM((N_HEADS, TA, TA), F32),
            pltpu.VMEM((N_HEADS, TA, TA), BF16),
        ],
        compiler_params=pltpu.CompilerParams(
            dimension_semantics=("parallel",),
            vmem_limit_bytes=VMEM_LIMIT),
        name="fox_attention",
    )(qt, eqt, k, ek, vt)


def _even_out_kernel(h_ref, ya_ref, yb_ref, wo_ref, gmp_ref, gfp_ref, wi_ref, wd_ref, gfq_ref, out_ref):
    m = _dot(ya_ref[0], wo_ref[0:D_HALF, :]) + _dot(yb_ref[0], wo_ref[D_HALF:, :])
    h1 = h_ref[0] + _rms(m, gmp_ref[...])
    out_ref[0] = _ffn(h1, gfp_ref[...], wi_ref, wd_ref, gfq_ref[...])


def _ffn_specs():
    return [
        _const_spec((1, D_MODEL)),
        _const_spec((D_MODEL, 2 * D_FF)),
        _const_spec((D_FF, D_MODEL)),
        _const_spec((1, D_MODEL)),
    ]


def _even_out(h, ya, yb, w_out, g_mix_post, ffn):
    bsz, seq, _ = h.shape
    return pl.pallas_call(
        _even_out_kernel,
        grid=(bsz, seq // TM),
        in_specs=[
            _tile_spec(TM, D_MODEL),
            _tile_spec(TM, D_HALF),
            _tile_spec(TM, D_HALF),
            _const_spec((D_MODEL, D_MODEL)),
            _const_spec((1, D_MODEL)),
        ] + _ffn_specs(),
        out_specs=_tile_spec(TM, D_MODEL),
        out_shape=jax.ShapeDtypeStruct(h.shape, F32),
        compiler_params=pltpu.CompilerParams(
            dimension_semantics=("parallel", "parallel"),
            vmem_limit_bytes=VMEM_LIMIT),
        name="even_out_ffn",
    )(h, ya, yb, w_out, g_mix_post, *ffn)


def _odd_kernel(h_ref, gpre_ref, w1_ref, b1_ref, wdw_ref, bdw_ref, lng_ref, lnb_ref, w2_ref, b2_ref,
                gmp_ref, gfp_ref, wi_ref, wd_ref, gfq_ref, out_ref, ubuf, cbuf):
    t = pl.program_id(1)
    halo = CONV_PITCH * CONF_HALO

    @pl.when(t == 0)
    def _():
        ubuf[:, 0:halo, :] = jnp.zeros((D_MODEL // LANES, halo, LANES), F32)

    h = h_ref[0]
    x = _rms(h, gpre_ref[...]).astype(BF16)

    base = CONF_HALO - (CONF_K - 1)
    for s2 in range(D_MODEL // CONV_SLAB):
        lanes2 = slice(s2 * CONV_SLAB, (s2 + 1) * CONV_SLAB)
        glanes2 = slice(D_MODEL + s2 * CONV_SLAB, D_MODEL + (s2 + 1) * CONV_SLAB)
        a = _dot(x, w1_ref[:, lanes2]) + b1_ref[:, lanes2]
        g = _dot(x, w1_ref[:, glanes2]) + b1_ref[:, glanes2]
        u = a * _sigmoid(g)
        for half in range(CONV_SLAB // LANES):
            s = s2 * (CONV_SLAB // LANES) + half
            lanes = slice(s * LANES, (s + 1) * LANES)
            ubuf[s, pl.ds(halo, TM, stride=CONV_PITCH), :] = u[:, half * LANES:(half + 1) * LANES]
            for r in range(0, TM, CONV_ROWS):
                y = jnp.broadcast_to(bdw_ref[:, lanes], (CONV_ROWS, LANES))
                for k in range(CONF_K):
                    win = ubuf[s, pl.ds(CONV_PITCH * (r + base + k), CONV_ROWS, stride=CONV_PITCH), :]
                    y = y + wdw_ref[k:k + 1, lanes] * win
                cbuf[r:r + CONV_ROWS, lanes] = y
            ubuf[s, pl.ds(0, CONF_HALO, stride=CONV_PITCH), :] = \
                ubuf[s, pl.ds(CONV_PITCH * TM, CONF_HALO, stride=CONV_PITCH), :]

    y = cbuf[...]
    mu = jnp.mean(y, axis=-1, keepdims=True)
    yc = y - mu
    var = jnp.mean(yc * yc, axis=-1, keepdims=True)
    yn = yc * lax.rsqrt(var + LN_EPS) * lng_ref[...] + lnb_ref[...]
    sw = (yn * _sigmoid(yn)).astype(BF16)
    m = _dot(sw, w2_ref[...]) + b2_ref[...]
    h1 = h + _rms(m, gmp_ref[...])
    out_ref[0] = _ffn(h1, gfp_ref[...], wi_ref, wd_ref, gfq_ref[...])


def _odd_layer(h, g_pre, w1, b1, wdw, bdw, lng, lnb, w2, b2, g_mix_post, ffn):
    bsz, seq, _ = h.shape
    return pl.pallas_call(
        _odd_kernel,
        grid=(bsz, seq // TM),
        in_specs=[
            _tile_spec(TM, D_MODEL),
            _const_spec((1, D_MODEL)),
            _const_spec((D_MODEL, 2 * D_MODEL)),
            _const_spec((1, 2 * D_MODEL)),
            _const_spec((CONF_K, D_MODEL)),
            _const_spec((1, D_MODEL)),
            _const_spec((1, D_MODEL)),
            _const_spec((1, D_MODEL)),
            _const_spec((D_MODEL, D_MODEL)),
            _const_spec((1, D_MODEL)),
            _const_spec((1, D_MODEL)),
        ] + _ffn_specs(),
        out_specs=_tile_spec(TM, D_MODEL),
        out_shape=jax.ShapeDtypeStruct(h.shape, F32),
        scratch_shapes=[
            pltpu.VMEM((D_MODEL // LANES, CONV_PITCH * (TM + CONF_HALO), LANES), F32),
            pltpu.VMEM((TM, D_MODEL), F32),
        ],
        compiler_params=pltpu.CompilerParams(
            dimension_semantics=("parallel", "arbitrary"),
            vmem_limit_bytes=VMEM_LIMIT),
        name="odd_conformer_ffn",
    )(h, g_pre, w1, b1, wdw, bdw, lng, lnb, w2, b2, g_mix_post, *ffn)


def _row(v):
    return v.reshape(1, -1)


def kernel(x, norm_mix_pre, norm_mix_post, norm_ffn_pre, norm_ffn_post, w_in, b_forget, w_short_conv, w_out, w_pw1, b_pw1, w_dw, b_dw, ln_g, ln_b, w_pw2, b_pw2, w_ffn_in, w_ffn_out):
    bsz, seq, d_model = x.shape
    assert d_model == D_MODEL and seq % TM == 0 and TM % TA == 0
    depth = norm_mix_pre.shape[0]
    consts = _bias_placement()
    n_main = 6 * D_HALF

    h = x
    for layer in range(depth):
        j = layer // 2
        ffn = (_row(norm_ffn_pre[layer]), w_ffn_in[layer].astype(BF16), w_ffn_out[layer].astype(BF16),
               _row(norm_ffn_post[layer]))
        if layer % 2 == 0:
            w_f = jnp.pad(w_in[j][:, n_main:], ((0, 0), (0, LANES - N_HEADS))).astype(BF16)
            b_f = jnp.pad(b_forget[j], (0, LANES - N_HEADS)).reshape(1, LANES)
            ya, qt, k, vt, ek, eqt = _even_in(h, _row(norm_mix_pre[layer]), w_in[j][:, :n_main].astype(BF16),
                                              w_f, b_f, w_short_conv[j], consts)
            yb = _fox(qt, eqt, k, ek, vt)
            h = _even_out(h, ya, yb, w_out[j].astype(BF16), _row(norm_mix_post[layer]), ffn)
        else:
            h = _odd_layer(h, _row(norm_mix_pre[layer]), w_pw1[j].astype(BF16), _row(b_pw1[j]),
                           w_dw[j], _row(b_dw[j]), _row(ln_g[j]), _row(ln_b[j]),
                           w_pw2[j].astype(BF16), _row(b_pw2[j]), _row(norm_mix_post[layer]), ffn)
    return h
```

```python
import numpy as np

import jax
import jax.numpy as jnp
from jax import lax
from jax.experimental import pallas as pl
from jax.experimental.pallas import tpu as pltpu

D_MODEL = 1024
HEAD_DIM = 64
N_HEADS = 8
D_HALF = D_MODEL // 2
SHORT_K = 3
CONF_K = 31
D_FF = 2816
D_IN_EVEN = 6 * D_HALF + N_HEADS
RMS_EPS = 1e-6
LN_EPS = 1e-5
LANES = 128
SUBLANES = 8
BF16_ROWS = 16

TM = 512
TA = 256
FF_CHUNK = 256
N_FF_CHUNKS = D_FF // FF_CHUNK
EX_W = 16
V_ROWS = HEAD_DIM + BF16_ROWS
ACC_LAG = 2
CONF_HALO = 32
CONV_SLAB = 256
CONV_PITCH = 2
CONV_ROWS = 128
VMEM_LIMIT = 56 * 1024 * 1024
NEG_BIG = -1e30

F32 = jnp.float32
BF16 = jnp.bfloat16


def _dot(a, b):
    return jnp.dot(a, b, preferred_element_type=F32)


def _rms(x, g):
    return x * lax.rsqrt(jnp.mean(x * x, axis=-1, keepdims=True) + RMS_EPS) * g


def _sigmoid(x):
    return 1.0 / (1.0 + jnp.exp(-x))


def _split3(x):
    hi = x.astype(BF16)
    r1 = x - hi.astype(F32)
    mid = r1.astype(BF16)
    lo = (r1 - mid.astype(F32)).astype(BF16)
    return hi, mid, lo


def _const_spec(shape):
    nd = len(shape)
    return pl.BlockSpec(shape, lambda *_: (0,) * nd, pipeline_mode=pl.Buffered(1))


def _layer_spec(shape, layer):
    nd = len(shape)
    return pl.BlockSpec((None,) + tuple(shape), lambda *_: (layer,) + (0,) * nd, pipeline_mode=pl.Buffered(1))


def _tile_spec(rows, cols):
    return pl.BlockSpec((1, rows, cols), lambda b, t: (b, t, 0))


def _ffn(h1, x, wi_ref, wo_ref, g_post):
    acc = jnp.zeros(h1.shape, F32)
    for c in range(N_FF_CHUNKS):
        lo = c * FF_CHUNK
        g = _dot(x, wi_ref[:, lo:lo + FF_CHUNK])
        u = _dot(x, wi_ref[:, D_FF + lo:D_FF + lo + FF_CHUNK])
        a = (g * _sigmoid(g) * u).astype(BF16)
        acc = acc + _dot(a, wo_ref[lo:lo + FF_CHUNK, :])
    return h1 + _rms(acc, g_post)


def _even_in_kernel(h_ref, g_ref, w_ref, wf_ref, bf_ref, cw_ref, tri_ref, pkq_ref, ck_ref, cq_ref,
                    ya_ref, qt_ref, k_ref, vt_ref, ek_ref, eqt_ref, pbuf, ccarry):
    t = pl.program_id(1)

    @pl.when(t == 0)
    def _():
        pbuf[0:SUBLANES, :] = jnp.zeros((SUBLANES, D_HALF), F32)
        ccarry[...] = jnp.zeros(ccarry.shape, F32)

    x = _rms(h_ref[0], g_ref[...]).astype(BF16)

    def proj(i):
        return _dot(x, w_ref[:, i * D_HALF:(i + 1) * D_HALF])

    z = _dot(x, wf_ref[...]) + bf_ref[...]
    lf = jnp.minimum(z, 0.0) - jnp.log1p(jnp.exp(-jnp.abs(z)))
    hi, mid, lo = _split3(lf)

    p = proj(1) * proj(2)
    pbuf[SUBLANES:SUBLANES + TM, :] = p
    p1 = pbuf[SUBLANES - 1:SUBLANES - 1 + TM, :]
    p2 = pbuf[SUBLANES - 2:SUBLANES - 2 + TM, :]
    conv = cw_ref[0:1, :] * p2 + cw_ref[1:2, :] * p1 + cw_ref[2:3, :] * p
    ya_ref[0] = (proj(0) * conv).astype(BF16)
    pbuf[0:SUBLANES, :] = pbuf[TM:TM + SUBLANES, :]

    tri = tri_ref[...]
    c2 = _dot(tri, jnp.concatenate([hi, mid], axis=1))
    c = (c2[:, :LANES] + c2[:, LANES:]) + _dot(tri, lo) + ccarry[0:1, :]
    ccarry[...] = jnp.broadcast_to(c[TM - 1:TM, :], ccarry.shape)
    hi, mid, lo = _split3(c)

    qt = (proj(3) * (HEAD_DIM ** -0.5)).T.astype(BF16)

    e2 = (_dot(hi, pkq_ref[0]) + _dot(mid, pkq_ref[1])) + _dot(lo, pkq_ref[2])
    ek_ref[0] = (e2[:, :LANES] + ck_ref[...]).astype(BF16)
    eqt = (e2[:, LANES:] + cq_ref[...]).T.astype(BF16)

    k_ref[0] = proj(4).astype(BF16)
    vt = proj(5).T
    fill = (lax.broadcasted_iota(jnp.int32, (BF16_ROWS, TM), 0) == 0).astype(F32)
    vt = jnp.concatenate(
        [piece for h in range(N_HEADS) for piece in (vt[h * HEAD_DIM:(h + 1) * HEAD_DIM], fill)],
        axis=0).astype(BF16)

    for s in range(TM // TA):
        cols = slice(s * TA, (s + 1) * TA)
        qt_ref[0, s] = qt[:, cols]
        vt_ref[0, s] = vt[:, cols]
        eqt_ref[0, s] = eqt[:, cols]


def _even_in(h, g_pre, w_in_all, layer, w_f, b_f, conv_w, consts):
    bsz, seq, _ = h.shape
    n_blk = seq // TA
    sub = TM // TA

    def blk_spec(rows):
        return pl.BlockSpec((1, sub, rows, TA), lambda b, t: (b, t, 0, 0))

    return pl.pallas_call(
        _even_in_kernel,
        grid=(bsz, seq // TM),
        in_specs=[
            _tile_spec(TM, D_MODEL),
            _const_spec((1, D_MODEL)),
            _layer_spec((D_MODEL, D_IN_EVEN), layer),
            _const_spec((D_MODEL, LANES)),
            _const_spec((1, LANES)),
            _const_spec((SHORT_K, D_HALF)),
            _const_spec((TM, TM)),
            _const_spec((3, LANES, 2 * LANES)),
            _const_spec((1, LANES)),
            _const_spec((1, LANES)),
        ],
        out_specs=[
            _tile_spec(TM, D_HALF),
            blk_spec(D_HALF),
            _tile_spec(TM, D_HALF),
            blk_spec(N_HEADS * V_ROWS),
            _tile_spec(TM, LANES),
            blk_spec(LANES),
        ],
        out_shape=[
            jax.ShapeDtypeStruct((bsz, seq, D_HALF), BF16),
            jax.ShapeDtypeStruct((bsz, n_blk, D_HALF, TA), BF16),
            jax.ShapeDtypeStruct((bsz, seq, D_HALF), BF16),
            jax.ShapeDtypeStruct((bsz, n_blk, N_HEADS * V_ROWS, TA), BF16),
            jax.ShapeDtypeStruct((bsz, seq, LANES), BF16),
            jax.ShapeDtypeStruct((bsz, n_blk, LANES, TA), BF16),
        ],
        scratch_shapes=[
            pltpu.VMEM((TM + 2 * SUBLANES, D_HALF), F32),
            pltpu.VMEM((SUBLANES, LANES), F32),
        ],
        compiler_params=pltpu.CompilerParams(
            dimension_semantics=("parallel", "arbitrary"),
            vmem_limit_bytes=VMEM_LIMIT),
        name="even_in_proj",
    )(h, g_pre, w_in_all, w_f, b_f, conv_w, *consts)


def _bias_placement():
    pk = np.zeros((3, LANES, LANES), np.float32)
    pq = np.zeros((3, LANES, LANES), np.float32)
    ck = np.zeros((1, LANES), np.float32)
    cq = np.zeros((1, LANES), np.float32)
    for h in range(N_HEADS):
        for i in range(3):
            pk[i, h, EX_W * h + i] = -1.0
            cq[0, EX_W * h + i] = 1.0
            pq[i, h, EX_W * h + 3 + i] = 1.0
            ck[0, EX_W * h + 3 + i] = 1.0
    tri = np.tril(np.ones((TM, TM), np.float32))
    return (jnp.asarray(tri, BF16), jnp.asarray(np.concatenate([pk, pq], axis=2), BF16),
            jnp.asarray(ck), jnp.asarray(cq))


def _fox_kernel(qt_ref, eqt_ref, k_ref, ek_ref, vt_ref, o_ref, qs, acc, m_sc, mx_sc, al_sc, s_sc, p_sc):
    n_blk = qt_ref.shape[1]

    def query_block(i, carry):
        _fox_query_block(i, qt_ref, eqt_ref, k_ref, ek_ref, vt_ref, o_ref, qs, acc, m_sc, mx_sc, al_sc, s_sc, p_sc)
        return carry

    lax.fori_loop(0, n_blk, query_block, 0)


def _fox_query_block(i, qt_ref, eqt_ref, k_ref, ek_ref, vt_ref, o_ref, qs, acc, m_sc, mx_sc, al_sc, s_sc, p_sc):
    zeros_q = jnp.zeros((HEAD_DIM, TA), BF16)
    for h in range(N_HEADS):
        qh = qt_ref[0, i, h * HEAD_DIM:(h + 1) * HEAD_DIM, :]
        pieces = [qh, zeros_q] if h % 2 == 0 else [zeros_q, qh]
        if h > 0:
            pieces.append(jnp.zeros((EX_W * h, TA), BF16))
        pieces.append(eqt_ref[0, i, EX_W * h:EX_W * (h + 1), :])
        if h < N_HEADS - 1:
            pieces.append(jnp.zeros((LANES - EX_W * (h + 1), TA), BF16))
        qs[h] = jnp.concatenate(pieces, axis=0)
        acc[h] = jnp.zeros((V_ROWS, TA), F32)
        m_sc[h] = jnp.full((SUBLANES, TA), NEG_BIG, F32)

    key_idx = lax.broadcasted_iota(jnp.int32, (TA, TA), 0)
    qry_idx = lax.broadcasted_iota(jnp.int32, (TA, TA), 1)
    causal = key_idx <= qry_idx

    def scores(h, kk, masked):
        st = _dot(kk, qs[h])
        if masked:
            st = jnp.where(causal, st, NEG_BIG)
        s_sc[h] = st
        mx_sc[h, 0:1, :] = jnp.max(st, axis=0, keepdims=True)

    def key_pairs(j):
        r0 = pl.multiple_of(j * TA, TA)
        ekb = ek_ref[0, pl.ds(r0, TA), :]
        return [jnp.concatenate([k_ref[0, pl.ds(r0, TA), p * LANES:(p + 1) * LANES], ekb], axis=1)
                for p in range(N_HEADS // 2)]

    def probs(h):
        m_old = m_sc[h, 0:1, :]
        m_new = jnp.maximum(m_old, mx_sc[h, 0:1, :])
        al_sc[h, 0:1, :] = jnp.exp(m_old - m_new)
        m_sc[h, 0:1, :] = m_new
        p_sc[h] = jnp.exp(s_sc[h] - m_new).astype(BF16)

    def accumulate(j, h):
        acc[h] = al_sc[h, 0:1, :] * acc[h] + _dot(vt_ref[0, j, h * V_ROWS:(h + 1) * V_ROWS, :], p_sc[h])

    kks = key_pairs(i)
    for h in range(N_HEADS):
        scores(h, kks[h // 2], True)

    def body(j, carry):
        prev = jnp.where(j == 0, i, j - 1)
        kks = key_pairs(j)
        for h in range(N_HEADS):
            probs(h)
            scores(h, kks[h // 2], False)
            if h >= ACC_LAG:
                accumulate(prev, h - ACC_LAG)
        for h in range(N_HEADS - ACC_LAG, N_HEADS):
            accumulate(prev, h)
        return carry

    lax.fori_loop(0, i, body, 0)
    last = jnp.maximum(i - 1, 0)
    for h in range(N_HEADS):
        probs(h)
        if h >= ACC_LAG:
            accumulate(last, h - ACC_LAG)
    for h in range(N_HEADS - ACC_LAG, N_HEADS):
        accumulate(last, h)

    outs = []
    for h in range(N_HEADS):
        a = acc[h]
        outs.append(a[0:HEAD_DIM] / a[HEAD_DIM:HEAD_DIM + 1])
    o_ref[0, pl.ds(pl.multiple_of(i * TA, TA), TA), :] = jnp.concatenate(outs, axis=0).T.astype(BF16)


def _fox(qt, eqt, k, ek, vt):
    bsz, seq, _ = k.shape
    n_blk = seq // TA
    return pl.pallas_call(
        _fox_kernel,
        grid=(bsz,),
        in_specs=[
            pl.BlockSpec((1, n_blk, D_HALF, TA), lambda b: (b, 0, 0, 0)),
            pl.BlockSpec((1, n_blk, LANES, TA), lambda b: (b, 0, 0, 0)),
            pl.BlockSpec((1, seq, D_HALF), lambda b: (b, 0, 0)),
            pl.BlockSpec((1, seq, LANES), lambda b: (b, 0, 0)),
            pl.BlockSpec((1, n_blk, N_HEADS * V_ROWS, TA), lambda b: (b, 0, 0, 0)),
        ],
        out_specs=pl.BlockSpec((1, seq, D_HALF), lambda b: (b, 0, 0)),
        out_shape=jax.ShapeDtypeStruct((bsz, seq, D_HALF), BF16),
        scratch_shapes=[
            pltpu.VMEM((N_HEADS, 2 * LANES, TA), BF16),
            pltpu.VMEM((N_HEADS, V_ROWS, TA), F32),
            pltpu.VMEM((N_HEADS, SUBLANES, TA), F32),
            pltpu.VMEM((N_HEADS, SUBLANES, TA), F32),
            pltpu.VMEM((N_HEADS, SUBLANES, TA), F32),
            pltpu.VMEM((N_HEADS, TA, TA), F32),
            pltpu.VMEM((N_HEADS, TA, TA), BF16),
        ],
        compiler_params=pltpu.CompilerParams(
            dimension_semantics=("parallel",),
            vmem_limit_bytes=VMEM_LIMIT),
        name="fox_attention",
    )(qt, eqt, k, ek, vt)


def _even_out_kernel(h_ref, ya_ref, yb_ref, wo_ref, gmp_ref, gfp_ref, wi_ref, wd_ref, gfq_ref, out_ref):
    h1s, xs = [], []
    for r in range(0, TM, TM // 2):
        rows = slice(r, r + TM // 2)
        m = _dot(ya_ref[0, rows, :], wo_ref[0:D_HALF, :]) + _dot(yb_ref[0, rows, :], wo_ref[D_HALF:, :])
        h1 = h_ref[0, rows, :] + _rms(m, gmp_ref[...])
        h1s.append(h1)
        xs.append(_rms(h1, gfp_ref[...]).astype(BF16))
    out_ref[0] = _ffn(jnp.concatenate(h1s, axis=0), jnp.concatenate(xs, axis=0), wi_ref, wd_ref, gfq_ref[...])


def _ffn_specs(layer):
    return [
        _const_spec((1, D_MODEL)),
        _layer_spec((D_MODEL, 2 * D_FF), layer),
        _layer_spec((D_FF, D_MODEL), layer),
        _const_spec((1, D_MODEL)),
    ]


def _even_out(h, ya, yb, w_out_all, j, g_mix_post, layer, ffn):
    bsz, seq, _ = h.shape
    return pl.pallas_call(
        _even_out_kernel,
        grid=(bsz, seq // TM),
        in_specs=[
            _tile_spec(TM, D_MODEL),
            _tile_spec(TM, D_HALF),
            _tile_spec(TM, D_HALF),
            _layer_spec((D_MODEL, D_MODEL), j),
            _const_spec((1, D_MODEL)),
        ] + _ffn_specs(layer),
        out_specs=_tile_spec(TM, D_MODEL),
        out_shape=jax.ShapeDtypeStruct(h.shape, F32),
        compiler_params=pltpu.CompilerParams(
            dimension_semantics=("parallel", "parallel"),
            vmem_limit_bytes=VMEM_LIMIT),
        name="even_out_ffn",
    )(h, ya, yb, w_out_all, g_mix_post, *ffn)


def _odd_kernel(h_ref, gpre_ref, w1_ref, b1_ref, wdw_ref, bdw_ref, lng_ref, lnb_ref, w2_ref, b2_ref,
                gmp_ref, gfp_ref, wi_ref, wd_ref, gfq_ref, out_ref, ubuf, cbuf):
    t = pl.program_id(1)
    halo = CONV_PITCH * CONF_HALO

    @pl.when(t == 0)
    def _():
        ubuf[:, 0:halo, :] = jnp.zeros((D_MODEL // LANES, halo, LANES), F32)

    h = h_ref[0]
    x = _rms(h, gpre_ref[...]).astype(BF16)

    base = CONF_HALO - (CONF_K - 1)
    for s2 in range(D_MODEL // CONV_SLAB):
        lanes2 = slice(s2 * CONV_SLAB, (s2 + 1) * CONV_SLAB)
        glanes2 = slice(D_MODEL + s2 * CONV_SLAB, D_MODEL + (s2 + 1) * CONV_SLAB)
        a = _dot(x, w1_ref[:, lanes2]) + b1_ref[:, lanes2]
        g = _dot(x, w1_ref[:, glanes2]) + b1_ref[:, glanes2]
        u = a * _sigmoid(g)
        for half in range(CONV_SLAB // LANES):
            s = s2 * (CONV_SLAB // LANES) + half
            lanes = slice(s * LANES, (s + 1) * LANES)
            ubuf[s, pl.ds(halo, TM, stride=CONV_PITCH), :] = u[:, half * LANES:(half + 1) * LANES]
            for r in range(0, TM, CONV_ROWS):
                y = jnp.broadcast_to(bdw_ref[:, lanes], (CONV_ROWS, LANES))
                for k in range(CONF_K):
                    win = ubuf[s, pl.ds(CONV_PITCH * (r + base + k), CONV_ROWS, stride=CONV_PITCH), :]
                    y = y + wdw_ref[k:k + 1, lanes] * win
                cbuf[r:r + CONV_ROWS, lanes] = y
            ubuf[s, pl.ds(0, CONF_HALO, stride=CONV_PITCH), :] = \
                ubuf[s, pl.ds(CONV_PITCH * TM, CONF_HALO, stride=CONV_PITCH), :]

    y = cbuf[...]
    mu = jnp.mean(y, axis=-1, keepdims=True)
    yc = y - mu
    var = jnp.mean(yc * yc, axis=-1, keepdims=True)
    yn = yc * lax.rsqrt(var + LN_EPS) * lng_ref[...] + lnb_ref[...]
    sw = (yn * _sigmoid(yn)).astype(BF16)
    m = _dot(sw, w2_ref[...]) + b2_ref[...]
    h1 = h + _rms(m, gmp_ref[...])
    out_ref[0] = _ffn(h1, _rms(h1, gfp_ref[...]).astype(BF16), wi_ref, wd_ref, gfq_ref[...])


def _odd_layer(h, g_pre, w1_all, b1, wdw, bdw, lng, lnb, w2_all, b2, g_mix_post, j, layer, ffn):
    bsz, seq, _ = h.shape
    return pl.pallas_call(
        _odd_kernel,
        grid=(bsz, seq // TM),
        in_specs=[
            _tile_spec(TM, D_MODEL),
            _const_spec((1, D_MODEL)),
            _layer_spec((D_MODEL, 2 * D_MODEL), j),
            _const_spec((1, 2 * D_MODEL)),
            _const_spec((CONF_K, D_MODEL)),
            _const_spec((1, D_MODEL)),
            _const_spec((1, D_MODEL)),
            _const_spec((1, D_MODEL)),
            _layer_spec((D_MODEL, D_MODEL), j),
            _const_spec((1, D_MODEL)),
            _const_spec((1, D_MODEL)),
        ] + _ffn_specs(layer),
        out_specs=_tile_spec(TM, D_MODEL),
        out_shape=jax.ShapeDtypeStruct(h.shape, F32),
        scratch_shapes=[
            pltpu.VMEM((D_MODEL // LANES, CONV_PITCH * (TM + CONF_HALO), LANES), F32),
            pltpu.VMEM((TM, D_MODEL), F32),
        ],
        compiler_params=pltpu.CompilerParams(
            dimension_semantics=("parallel", "arbitrary"),
            vmem_limit_bytes=VMEM_LIMIT),
        name="odd_conformer_ffn",
    )(h, g_pre, w1_all, b1, wdw, bdw, lng, lnb, w2_all, b2, g_mix_post, *ffn)


def _row(v):
    return v.reshape(1, -1)


def kernel(x, norm_mix_pre, norm_mix_post, norm_ffn_pre, norm_ffn_post, w_in, b_forget, w_short_conv, w_out, w_pw1, b_pw1, w_dw, b_dw, ln_g, ln_b, w_pw2, b_pw2, w_ffn_in, w_ffn_out):
    bsz, seq, d_model = x.shape
    assert d_model == D_MODEL and seq % TM == 0 and TM % TA == 0
    depth = norm_mix_pre.shape[0]
    consts = _bias_placement()
    n_main = 6 * D_HALF
    w_in_b, w_out_b, w_pw1_b, w_pw2_b = (w.astype(BF16) for w in (w_in, w_out, w_pw1, w_pw2))
    w_ffn_in_b, w_ffn_out_b = w_ffn_in.astype(BF16), w_ffn_out.astype(BF16)

    h = x
    for layer in range(depth):
        j = layer // 2
        ffn = (_row(norm_ffn_pre[layer]), w_ffn_in_b, w_ffn_out_b, _row(norm_ffn_post[layer]))
        if layer % 2 == 0:
            w_f = jnp.pad(w_in[j][:, n_main:], ((0, 0), (0, LANES - N_HEADS))).astype(BF16)
            b_f = jnp.pad(b_forget[j], (0, LANES - N_HEADS)).reshape(1, LANES)
            ya, qt, k, vt, ek, eqt = _even_in(h, _row(norm_mix_pre[layer]), w_in_b, j,
                                              w_f, b_f, w_short_conv[j], consts)
            yb = _fox(qt, eqt, k, ek, vt)
            h = _even_out(h, ya, yb, w_out_b, j, _row(norm_mix_post[layer]), layer, ffn)
        else:
            h = _odd_layer(h, _row(norm_mix_pre[layer]), w_pw1_b, _row(b_pw1[j]),
                           w_dw[j], _row(b_dw[j]), _row(ln_g[j]), _row(ln_b[j]),
                           w_pw2_b, _row(b_pw2[j]), _row(norm_mix_post[layer]), j, layer, ffn)
    return h
```

```python
import numpy as np

import jax
import jax.numpy as jnp
from jax import lax
from jax.experimental import pallas as pl
from jax.experimental.pallas import tpu as pltpu

D_MODEL = 1024
HEAD_DIM = 64
N_HEADS = 8
D_HALF = D_MODEL // 2
SHORT_K = 3
CONF_K = 31
D_FF = 2816
D_IN_EVEN = 6 * D_HALF + N_HEADS
RMS_EPS = 1e-6
LN_EPS = 1e-5
LANES = 128
SUBLANES = 8
BF16_ROWS = 16

TM = 512
TA = 256
FF_CHUNK = 256
N_FF_CHUNKS = D_FF // FF_CHUNK
EX_W = 16
V_ROWS = HEAD_DIM + BF16_ROWS
ACC_LAG = 2
CONF_HALO = 32
CONV_SLAB = 256
CONV_PITCH = 2
CONV_ROWS = 128
VMEM_LIMIT = 56 * 1024 * 1024
NEG_BIG = -1e30
LOG2E = 1.4426950408889634

F32 = jnp.float32
BF16 = jnp.bfloat16


def _dot(a, b):
    return jnp.dot(a, b, preferred_element_type=F32)


def _rms(x, g):
    return x * lax.rsqrt(jnp.mean(x * x, axis=-1, keepdims=True) + RMS_EPS) * g


def _sigmoid(x):
    return 1.0 / (1.0 + jnp.exp(-x))


def _split3(x):
    hi = x.astype(BF16)
    r1 = x - hi.astype(F32)
    mid = r1.astype(BF16)
    lo = (r1 - mid.astype(F32)).astype(BF16)
    return hi, mid, lo


def _const_spec(shape):
    nd = len(shape)
    return pl.BlockSpec(shape, lambda *_: (0,) * nd, pipeline_mode=pl.Buffered(1))


def _layer_spec(shape, layer):
    nd = len(shape)
    return pl.BlockSpec((None,) + tuple(shape), lambda *_: (layer,) + (0,) * nd, pipeline_mode=pl.Buffered(1))


def _tile_spec(rows, cols):
    return pl.BlockSpec((1, rows, cols), lambda b, t: (b, t, 0))


def _ffn(h1, x, wi_ref, wo_ref, g_post):
    acc = jnp.zeros(h1.shape, F32)
    for c in range(N_FF_CHUNKS):
        lo = c * FF_CHUNK
        g = _dot(x, wi_ref[:, lo:lo + FF_CHUNK])
        u = _dot(x, wi_ref[:, D_FF + lo:D_FF + lo + FF_CHUNK])
        a = (g * _sigmoid(g) * u).astype(BF16)
        acc = acc + _dot(a, wo_ref[lo:lo + FF_CHUNK, :])
    return h1 + _rms(acc, g_post)


def _even_in_kernel(h_ref, g_ref, w_ref, wf_ref, bf_ref, cw_ref, tri_ref, pkq_ref, ck_ref, cq_ref,
                    ya_ref, qt_ref, k_ref, vt_ref, ek_ref, eqt_ref, pbuf, ccarry):
    t = pl.program_id(1)

    @pl.when(t == 0)
    def _():
        pbuf[0:SUBLANES, :] = jnp.zeros((SUBLANES, D_HALF), F32)
        ccarry[...] = jnp.zeros(ccarry.shape, F32)

    x = _rms(h_ref[0], g_ref[...]).astype(BF16)

    def proj(i):
        return _dot(x, w_ref[:, i * D_HALF:(i + 1) * D_HALF])

    z = _dot(x, wf_ref[...]) + bf_ref[...]
    lf = jnp.minimum(z, 0.0) - jnp.log1p(jnp.exp(-jnp.abs(z)))
    hi, mid, lo = _split3(lf)

    p = proj(1) * proj(2)
    pbuf[SUBLANES:SUBLANES + TM, :] = p
    p1 = pbuf[SUBLANES - 1:SUBLANES - 1 + TM, :]
    p2 = pbuf[SUBLANES - 2:SUBLANES - 2 + TM, :]
    conv = cw_ref[0:1, :] * p2 + cw_ref[1:2, :] * p1 + cw_ref[2:3, :] * p
    ya_ref[0] = (proj(0) * conv).astype(BF16)
    pbuf[0:SUBLANES, :] = pbuf[TM:TM + SUBLANES, :]

    tri = tri_ref[...]
    c2 = _dot(tri, jnp.concatenate([hi, mid], axis=1))
    c = (c2[:, :LANES] + c2[:, LANES:]) + _dot(tri, lo) + ccarry[0:1, :]
    ccarry[...] = jnp.broadcast_to(c[TM - 1:TM, :], ccarry.shape)
    hi, mid, lo = _split3(c * LOG2E)

    qt = (proj(3) * (HEAD_DIM ** -0.5 * LOG2E)).T.astype(BF16)

    e2 = (_dot(hi, pkq_ref[0]) + _dot(mid, pkq_ref[1])) + _dot(lo, pkq_ref[2])
    ek_ref[0] = (e2[:, :LANES] + ck_ref[...]).astype(BF16)
    eqt = (e2[:, LANES:] + cq_ref[...]).T.astype(BF16)

    k_ref[0] = proj(4).astype(BF16)
    vt = proj(5).T
    fill = (lax.broadcasted_iota(jnp.int32, (BF16_ROWS, TM), 0) == 0).astype(F32)
    vt = jnp.concatenate(
        [piece for h in range(N_HEADS) for piece in (vt[h * HEAD_DIM:(h + 1) * HEAD_DIM], fill)],
        axis=0).astype(BF16)

    for s in range(TM // TA):
        cols = slice(s * TA, (s + 1) * TA)
        qt_ref[0, s] = qt[:, cols]
        vt_ref[0, s] = vt[:, cols]
        eqt_ref[0, s] = eqt[:, cols]


def _even_in(h, g_pre, w_in_all, layer, w_f, b_f, conv_w, consts):
    bsz, seq, _ = h.shape
    n_blk = seq // TA
    sub = TM // TA

    def blk_spec(rows):
        return pl.BlockSpec((1, sub, rows, TA), lambda b, t: (b, t, 0, 0))

    return pl.pallas_call(
        _even_in_kernel,
        grid=(bsz, seq // TM),
        in_specs=[
            _tile_spec(TM, D_MODEL),
            _const_spec((1, D_MODEL)),
            _layer_spec((D_MODEL, D_IN_EVEN), layer),
            _const_spec((D_MODEL, LANES)),
            _const_spec((1, LANES)),
            _const_spec((SHORT_K, D_HALF)),
            _const_spec((TM, TM)),
            _const_spec((3, LANES, 2 * LANES)),
            _const_spec((1, LANES)),
            _const_spec((1, LANES)),
        ],
        out_specs=[
            _tile_spec(TM, D_HALF),
            blk_spec(D_HALF),
            _tile_spec(TM, D_HALF),
            blk_spec(N_HEADS * V_ROWS),
            _tile_spec(TM, LANES),
            blk_spec(LANES),
        ],
        out_shape=[
            jax.ShapeDtypeStruct((bsz, seq, D_HALF), BF16),
            jax.ShapeDtypeStruct((bsz, n_blk, D_HALF, TA), BF16),
            jax.ShapeDtypeStruct((bsz, seq, D_HALF), BF16),
            jax.ShapeDtypeStruct((bsz, n_blk, N_HEADS * V_ROWS, TA), BF16),
            jax.ShapeDtypeStruct((bsz, seq, LANES), BF16),
            jax.ShapeDtypeStruct((bsz, n_blk, LANES, TA), BF16),
        ],
        scratch_shapes=[
            pltpu.VMEM((TM + 2 * SUBLANES, D_HALF), F32),
            pltpu.VMEM((SUBLANES, LANES), F32),
        ],
        compiler_params=pltpu.CompilerParams(
            dimension_semantics=("parallel", "arbitrary"),
            vmem_limit_bytes=VMEM_LIMIT),
        name="even_in_proj",
    )(h, g_pre, w_in_all, w_f, b_f, conv_w, *consts)


def _bias_placement():
    pk = np.zeros((3, LANES, LANES), np.float32)
    pq = np.zeros((3, LANES, LANES), np.float32)
    ck = np.zeros((1, LANES), np.float32)
    cq = np.zeros((1, LANES), np.float32)
    for h in range(N_HEADS):
        for i in range(3):
            pk[i, h, EX_W * h + i] = -1.0
            cq[0, EX_W * h + i] = 1.0
            pq[i, h, EX_W * h + 3 + i] = 1.0
            ck[0, EX_W * h + 3 + i] = 1.0
    tri = np.tril(np.ones((TM, TM), np.float32))
    return (jnp.asarray(tri, BF16), jnp.asarray(np.concatenate([pk, pq], axis=2), BF16),
            jnp.asarray(ck), jnp.asarray(cq))


def _fox_kernel(qt_ref, eqt_ref, k_ref, ek_ref, vt_ref, o_ref, qs, acc, m_sc, mx_sc, al_sc, s_sc, p_sc):
    n_blk = qt_ref.shape[1]

    def query_block(i, carry):
        _fox_query_block(i, qt_ref, eqt_ref, k_ref, ek_ref, vt_ref, o_ref, qs, acc, m_sc, mx_sc, al_sc, s_sc, p_sc)
        return carry

    lax.fori_loop(0, n_blk, query_block, 0)


def _fox_query_block(i, qt_ref, eqt_ref, k_ref, ek_ref, vt_ref, o_ref, qs, acc, m_sc, mx_sc, al_sc, s_sc, p_sc):
    zeros_q = jnp.zeros((HEAD_DIM, TA), BF16)
    for h in range(N_HEADS):
        qh = qt_ref[0, i, h * HEAD_DIM:(h + 1) * HEAD_DIM, :]
        pieces = [qh, zeros_q] if h % 2 == 0 else [zeros_q, qh]
        if h > 0:
            pieces.append(jnp.zeros((EX_W * h, TA), BF16))
        pieces.append(eqt_ref[0, i, EX_W * h:EX_W * (h + 1), :])
        if h < N_HEADS - 1:
            pieces.append(jnp.zeros((LANES - EX_W * (h + 1), TA), BF16))
        qs[h] = jnp.concatenate(pieces, axis=0)
        acc[h] = jnp.zeros((V_ROWS, TA), F32)
        m_sc[h] = jnp.full((SUBLANES, TA), NEG_BIG, F32)

    key_idx = lax.broadcasted_iota(jnp.int32, (TA, TA), 0)
    qry_idx = lax.broadcasted_iota(jnp.int32, (TA, TA), 1)
    causal = key_idx <= qry_idx

    def scores(h, kk, masked):
        st = _dot(kk, qs[h])
        if masked:
            st = jnp.where(causal, st, NEG_BIG)
        s_sc[h] = st
        mx_sc[h, 0:1, :] = jnp.max(st, axis=0, keepdims=True)

    def key_pairs(j):
        r0 = pl.multiple_of(j * TA, TA)
        ekb = ek_ref[0, pl.ds(r0, TA), :]
        return [jnp.concatenate([k_ref[0, pl.ds(r0, TA), p * LANES:(p + 1) * LANES], ekb], axis=1)
                for p in range(N_HEADS // 2)]

    def probs(h):
        m_old = m_sc[h, 0:1, :]
        m_new = jnp.maximum(m_old, mx_sc[h, 0:1, :])
        al_sc[h, 0:1, :] = jnp.exp2(m_old - m_new)
        m_sc[h, 0:1, :] = m_new
        p_sc[h] = jnp.exp2(s_sc[h] - m_new).astype(BF16)

    def accumulate(j, h):
        acc[h] = al_sc[h, 0:1, :] * acc[h] + _dot(vt_ref[0, j, h * V_ROWS:(h + 1) * V_ROWS, :], p_sc[h])

    kks = key_pairs(i)
    for h in range(N_HEADS):
        scores(h, kks[h // 2], True)

    def body(j, carry):
        prev = jnp.where(j == 0, i, j - 1)
        kks = key_pairs(j)
        for h in range(N_HEADS):
            probs(h)
            scores(h, kks[h // 2], False)
            if h >= ACC_LAG:
                accumulate(prev, h - ACC_LAG)
        for h in range(N_HEADS - ACC_LAG, N_HEADS):
            accumulate(prev, h)
        return carry

    lax.fori_loop(0, i, body, 0)
    last = jnp.maximum(i - 1, 0)
    for h in range(N_HEADS):
        probs(h)
        if h >= ACC_LAG:
            accumulate(last, h - ACC_LAG)
    for h in range(N_HEADS - ACC_LAG, N_HEADS):
        accumulate(last, h)

    outs = []
    for h in range(N_HEADS):
        a = acc[h]
        outs.append(a[0:HEAD_DIM] / a[HEAD_DIM:HEAD_DIM + 1])
    o_ref[0, pl.ds(pl.multiple_of(i * TA, TA), TA), :] = jnp.concatenate(outs, axis=0).T.astype(BF16)


def _fox(qt, eqt, k, ek, vt):
    bsz, seq, _ = k.shape
    n_blk = seq // TA
    return pl.pallas_call(
        _fox_kernel,
        grid=(bsz,),
        in_specs=[
            pl.BlockSpec((1, n_blk, D_HALF, TA), lambda b: (b, 0, 0, 0)),
            pl.BlockSpec((1, n_blk, LANES, TA), lambda b: (b, 0, 0, 0)),
            pl.BlockSpec((1, seq, D_HALF), lambda b: (b, 0, 0)),
            pl.BlockSpec((1, seq, LANES), lambda b: (b, 0, 0)),
            pl.BlockSpec((1, n_blk, N_HEADS * V_ROWS, TA), lambda b: (b, 0, 0, 0)),
        ],
        out_specs=pl.BlockSpec((1, seq, D_HALF), lambda b: (b, 0, 0)),
        out_shape=jax.ShapeDtypeStruct((bsz, seq, D_HALF), BF16),
        scratch_shapes=[
            pltpu.VMEM((N_HEADS, 2 * LANES, TA), BF16),
            pltpu.VMEM((N_HEADS, V_ROWS, TA), F32),
            pltpu.VMEM((N_HEADS, SUBLANES, TA), F32),
            pltpu.VMEM((N_HEADS, SUBLANES, TA), F32),
            pltpu.VMEM((N_HEADS, SUBLANES, TA), F32),
            pltpu.VMEM((N_HEADS, TA, TA), F32),
            pltpu.VMEM((N_HEADS, TA, TA), BF16),
        ],
        compiler_params=pltpu.CompilerParams(
            dimension_semantics=("parallel",),
            vmem_limit_bytes=VMEM_LIMIT),
        name="fox_attention",
    )(qt, eqt, k, ek, vt)


def _even_out_kernel(h_ref, ya_ref, yb_ref, wo_ref, gmp_ref, gfp_ref, wi_ref, wd_ref, gfq_ref, out_ref):
    h1s, xs = [], []
    for r in range(0, TM, TM // 2):
        rows = slice(r, r + TM // 2)
        m = _dot(ya_ref[0, rows, :], wo_ref[0:D_HALF, :]) + _dot(yb_ref[0, rows, :], wo_ref[D_HALF:, :])
        h1 = h_ref[0, rows, :] + _rms(m, gmp_ref[...])
        h1s.append(h1)
        xs.append(_rms(h1, gfp_ref[...]).astype(BF16))
    out_ref[0] = _ffn(jnp.concatenate(h1s, axis=0), jnp.concatenate(xs, axis=0), wi_ref, wd_ref, gfq_ref[...])


def _ffn_specs(layer):
    return [
        _const_spec((1, D_MODEL)),
        _layer_spec((D_MODEL, 2 * D_FF), layer),
        _layer_spec((D_FF, D_MODEL), layer),
        _const_spec((1, D_MODEL)),
    ]


def _even_out(h, ya, yb, w_out_all, j, g_mix_post, layer, ffn):
    bsz, seq, _ = h.shape
    return pl.pallas_call(
        _even_out_kernel,
        grid=(bsz, seq // TM),
        in_specs=[
            _tile_spec(TM, D_MODEL),
            _tile_spec(TM, D_HALF),
            _tile_spec(TM, D_HALF),
            _layer_spec((D_MODEL, D_MODEL), j),
            _const_spec((1, D_MODEL)),
        ] + _ffn_specs(layer),
        out_specs=_tile_spec(TM, D_MODEL),
        out_shape=jax.ShapeDtypeStruct(h.shape, F32),
        compiler_params=pltpu.CompilerParams(
            dimension_semantics=("parallel", "parallel"),
            vmem_limit_bytes=VMEM_LIMIT),
        name="even_out_ffn",
    )(h, ya, yb, w_out_all, g_mix_post, *ffn)


def _odd_kernel(h_ref, gpre_ref, w1_ref, b1_ref, wdw_ref, bdw_ref, lng_ref, lnb_ref, w2_ref, b2_ref,
                gmp_ref, gfp_ref, wi_ref, wd_ref, gfq_ref, out_ref, ubuf, cbuf):
    t = pl.program_id(1)
    halo = CONV_PITCH * CONF_HALO

    @pl.when(t == 0)
    def _():
        ubuf[:, 0:halo, :] = jnp.zeros((D_MODEL // LANES, halo, LANES), F32)

    h = h_ref[0]
    x = _rms(h, gpre_ref[...]).astype(BF16)

    base = CONF_HALO - (CONF_K - 1)
    for s2 in range(D_MODEL // CONV_SLAB):
        lanes2 = slice(s2 * CONV_SLAB, (s2 + 1) * CONV_SLAB)
        glanes2 = slice(D_MODEL + s2 * CONV_SLAB, D_MODEL + (s2 + 1) * CONV_SLAB)
        a = _dot(x, w1_ref[:, lanes2]) + b1_ref[:, lanes2]
        g = _dot(x, w1_ref[:, glanes2]) + b1_ref[:, glanes2]
        u = a * _sigmoid(g)
        for half in range(CONV_SLAB // LANES):
            s = s2 * (CONV_SLAB // LANES) + half
            lanes = slice(s * LANES, (s + 1) * LANES)
            ubuf[s, pl.ds(halo, TM, stride=CONV_PITCH), :] = u[:, half * LANES:(half + 1) * LANES]
            for r in range(0, TM, CONV_ROWS):
                y = jnp.broadcast_to(bdw_ref[:, lanes], (CONV_ROWS, LANES))
                for k in range(CONF_K):
                    win = ubuf[s, pl.ds(CONV_PITCH * (r + base + k), CONV_ROWS, stride=CONV_PITCH), :]
                    y = y + wdw_ref[k:k + 1, lanes] * win
                cbuf[r:r + CONV_ROWS, lanes] = y
            ubuf[s, pl.ds(0, CONF_HALO, stride=CONV_PITCH), :] = \
                ubuf[s, pl.ds(CONV_PITCH * TM, CONF_HALO, stride=CONV_PITCH), :]

    y = cbuf[...]
    mu = jnp.mean(y, axis=-1, keepdims=True)
    yc = y - mu
    var = jnp.mean(yc * yc, axis=-1, keepdims=True)
    yn = yc * lax.rsqrt(var + LN_EPS) * lng_ref[...] + lnb_ref[...]
    sw = (yn * _sigmoid(yn)).astype(BF16)
    m = _dot(sw, w2_ref[...]) + b2_ref[...]
    h1 = h + _rms(m, gmp_ref[...])
    out_ref[0] = _ffn(h1, _rms(h1, gfp_ref[...]).astype(BF16), wi_ref, wd_ref, gfq_ref[...])


def _odd_layer(h, g_pre, w1_all, b1, wdw, bdw, lng, lnb, w2_all, b2, g_mix_post, j, layer, ffn):
    bsz, seq, _ = h.shape
    return pl.pallas_call(
        _odd_kernel,
        grid=(bsz, seq // TM),
        in_specs=[
            _tile_spec(TM, D_MODEL),
            _const_spec((1, D_MODEL)),
            _layer_spec((D_MODEL, 2 * D_MODEL), j),
            _const_spec((1, 2 * D_MODEL)),
            _const_spec((CONF_K, D_MODEL)),
            _const_spec((1, D_MODEL)),
            _const_spec((1, D_MODEL)),
            _const_spec((1, D_MODEL)),
            _layer_spec((D_MODEL, D_MODEL), j),
            _const_spec((1, D_MODEL)),
            _const_spec((1, D_MODEL)),
        ] + _ffn_specs(layer),
        out_specs=_tile_spec(TM, D_MODEL),
        out_shape=jax.ShapeDtypeStruct(h.shape, F32),
        scratch_shapes=[
            pltpu.VMEM((D_MODEL // LANES, CONV_PITCH * (TM + CONF_HALO), LANES), F32),
            pltpu.VMEM((TM, D_MODEL), F32),
        ],
        compiler_params=pltpu.CompilerParams(
            dimension_semantics=("parallel", "arbitrary"),
            vmem_limit_bytes=VMEM_LIMIT),
        name="odd_conformer_ffn",
    )(h, g_pre, w1_all, b1, wdw, bdw, lng, lnb, w2_all, b2, g_mix_post, *ffn)


def _row(v):
    return v.reshape(1, -1)


def kernel(x, norm_mix_pre, norm_mix_post, norm_ffn_pre, norm_ffn_post, w_in, b_forget, w_short_conv, w_out, w_pw1, b_pw1, w_dw, b_dw, ln_g, ln_b, w_pw2, b_pw2, w_ffn_in, w_ffn_out):
    bsz, seq, d_model = x.shape
    assert d_model == D_MODEL and seq % TM == 0 and TM % TA == 0
    depth = norm_mix_pre.shape[0]
    consts = _bias_placement()
    n_main = 6 * D_HALF
    w_in_b, w_out_b, w_pw1_b, w_pw2_b = (w.astype(BF16) for w in (w_in, w_out, w_pw1, w_pw2))
    w_ffn_in_b, w_ffn_out_b = w_ffn_in.astype(BF16), w_ffn_out.astype(BF16)

    h = x
    for layer in range(depth):
        j = layer // 2
        ffn = (_row(norm_ffn_pre[layer]), w_ffn_in_b, w_ffn_out_b, _row(norm_ffn_post[layer]))
        if layer % 2 == 0:
            w_f = jnp.pad(w_in[j][:, n_main:], ((0, 0), (0, LANES - N_HEADS))).astype(BF16)
            b_f = jnp.pad(b_forget[j], (0, LANES - N_HEADS)).reshape(1, LANES)
            ya, qt, k, vt, ek, eqt = _even_in(h, _row(norm_mix_pre[layer]), w_in_b, j,
                                              w_f, b_f, w_short_conv[j], consts)
            yb = _fox(qt, eqt, k, ek, vt)
            h = _even_out(h, ya, yb, w_out_b, j, _row(norm_mix_post[layer]), layer, ffn)
        else:
            h = _odd_layer(h, _row(norm_mix_pre[layer]), w_pw1_b, _row(b_pw1[j]),
                           w_dw[j], _row(b_dw[j]), _row(ln_g[j]), _row(ln_b[j]),
                           w_pw2_b, _row(b_pw2[j]), _row(norm_mix_post[layer]), j, layer, ffn)
    return h
```

```python
import numpy as np

import jax
import jax.numpy as jnp
from jax import lax
from jax.experimental import pallas as pl
from jax.experimental.pallas import tpu as pltpu

D_MODEL = 1024
HEAD_DIM = 64
N_HEADS = 8
D_HALF = D_MODEL // 2
SHORT_K = 3
CONF_K = 31
D_FF = 2816
D_IN_EVEN = 6 * D_HALF + N_HEADS
RMS_EPS = 1e-6
LN_EPS = 1e-5
LANES = 128
SUBLANES = 8
BF16_ROWS = 16

TM = 512
TA = 256
FF_CHUNK = 256
N_FF_CHUNKS = D_FF // FF_CHUNK
EX_W = 16
V_ROWS = HEAD_DIM + BF16_ROWS
ACC_LAG = 2
CONF_HALO = 32
CONV_SLAB = 256
CONV_PITCH = 2
CONV_ROWS = 128
VMEM_LIMIT = 56 * 1024 * 1024
NEG_BIG = -1e30
LOG2E = 1.4426950408889634

F32 = jnp.float32
BF16 = jnp.bfloat16


def _dot(a, b):
    return jnp.dot(a, b, preferred_element_type=F32)


def _rms(x, g):
    return x * lax.rsqrt(jnp.mean(x * x, axis=-1, keepdims=True) + RMS_EPS) * g


def _sigmoid(x):
    return 1.0 / (1.0 + jnp.exp(-x))


def _split3(x):
    hi = x.astype(BF16)
    r1 = x - hi.astype(F32)
    mid = r1.astype(BF16)
    lo = (r1 - mid.astype(F32)).astype(BF16)
    return hi, mid, lo


def _const_spec(shape):
    nd = len(shape)
    return pl.BlockSpec(shape, lambda *_: (0,) * nd, pipeline_mode=pl.Buffered(1))


def _layer_spec(shape, layer):
    nd = len(shape)
    return pl.BlockSpec((None,) + tuple(shape), lambda *_: (layer,) + (0,) * nd, pipeline_mode=pl.Buffered(1))


def _tile_spec(rows, cols):
    return pl.BlockSpec((1, rows, cols), lambda b, t: (b, t, 0))


def _ffn(h1, x, wi_ref, wo_ref, g_post):
    acc = jnp.zeros(h1.shape, F32)
    for c in range(N_FF_CHUNKS):
        lo = c * FF_CHUNK
        g = _dot(x, wi_ref[:, lo:lo + FF_CHUNK])
        u = _dot(x, wi_ref[:, D_FF + lo:D_FF + lo + FF_CHUNK])
        a = (g * _sigmoid(g) * u).astype(BF16)
        acc = acc + _dot(a, wo_ref[lo:lo + FF_CHUNK, :])
    return h1 + _rms(acc, g_post)


def _even_in_kernel(h_ref, g_ref, w_ref, wf_ref, bf_ref, cw_ref, tri_ref, pkq_ref, ck_ref, cq_ref,
                    ya_ref, qt_ref, k_ref, vt_ref, ek_ref, eqt_ref, pbuf, ccarry):
    t = pl.program_id(1)

    @pl.when(t == 0)
    def _():
        pbuf[0:SUBLANES, :] = jnp.zeros((SUBLANES, D_HALF), F32)
        ccarry[...] = jnp.zeros(ccarry.shape, F32)

    x = _rms(h_ref[0], g_ref[...]).astype(BF16)

    def proj(i):
        return _dot(x, w_ref[:, i * D_HALF:(i + 1) * D_HALF])

    z = _dot(x, wf_ref[...]) + bf_ref[...]
    lf = jnp.minimum(z, 0.0) - jnp.log1p(jnp.exp(-jnp.abs(z)))
    hi, mid, lo = _split3(lf)

    p = proj(1) * proj(2)
    pbuf[SUBLANES:SUBLANES + TM, :] = p
    p1 = pbuf[SUBLANES - 1:SUBLANES - 1 + TM, :]
    p2 = pbuf[SUBLANES - 2:SUBLANES - 2 + TM, :]
    conv = cw_ref[0:1, :] * p2 + cw_ref[1:2, :] * p1 + cw_ref[2:3, :] * p
    ya_ref[0] = (proj(0) * conv).astype(BF16)
    pbuf[0:SUBLANES, :] = pbuf[TM:TM + SUBLANES, :]

    tri = tri_ref[...]
    c2 = _dot(tri, jnp.concatenate([hi, mid], axis=1))
    c = (c2[:, :LANES] + c2[:, LANES:]) + _dot(tri, lo) + ccarry[0:1, :]
    ccarry[...] = jnp.broadcast_to(c[TM - 1:TM, :], ccarry.shape)
    hi, mid, lo = _split3(c * LOG2E)

    qt = (proj(3) * (HEAD_DIM ** -0.5 * LOG2E)).T.astype(BF16)

    e2 = (_dot(hi, pkq_ref[0]) + _dot(mid, pkq_ref[1])) + _dot(lo, pkq_ref[2])
    ek_ref[0] = (e2[:, :LANES] + ck_ref[...]).astype(BF16)
    eqt = (e2[:, LANES:] + cq_ref[...]).T.astype(BF16)

    k_ref[0] = proj(4).astype(BF16)
    vt = proj(5).T
    fill = (lax.broadcasted_iota(jnp.int32, (BF16_ROWS, TM), 0) == 0).astype(F32)
    vt = jnp.concatenate(
        [piece for h in range(N_HEADS) for piece in (vt[h * HEAD_DIM:(h + 1) * HEAD_DIM], fill)],
        axis=0).astype(BF16)

    for s in range(TM // TA):
        cols = slice(s * TA, (s + 1) * TA)
        qt_ref[0, s] = qt[:, cols]
        vt_ref[0, s] = vt[:, cols]
        eqt_ref[0, s] = eqt[:, cols]


def _even_in(h, g_pre, w_in_all, layer, w_f, b_f, conv_w, consts):
    bsz, seq, _ = h.shape
    n_blk = seq // TA
    sub = TM // TA

    def blk_spec(rows):
        return pl.BlockSpec((1, sub, rows, TA), lambda b, t: (b, t, 0, 0))

    return pl.pallas_call(
        _even_in_kernel,
        grid=(bsz, seq // TM),
        in_specs=[
            _tile_spec(TM, D_MODEL),
            _const_spec((1, D_MODEL)),
            _layer_spec((D_MODEL, D_IN_EVEN), layer),
            _const_spec((D_MODEL, LANES)),
            _const_spec((1, LANES)),
            _const_spec((SHORT_K, D_HALF)),
            _const_spec((TM, TM)),
            _const_spec((3, LANES, 2 * LANES)),
            _const_spec((1, LANES)),
            _const_spec((1, LANES)),
        ],
        out_specs=[
            _tile_spec(TM, D_HALF),
            blk_spec(D_HALF),
            _tile_spec(TM, D_HALF),
            blk_spec(N_HEADS * V_ROWS),
            _tile_spec(TM, LANES),
            blk_spec(LANES),
        ],
        out_shape=[
            jax.ShapeDtypeStruct((bsz, seq, D_HALF), BF16),
            jax.ShapeDtypeStruct((bsz, n_blk, D_HALF, TA), BF16),
            jax.ShapeDtypeStruct((bsz, seq, D_HALF), BF16),
            jax.ShapeDtypeStruct((bsz, n_blk, N_HEADS * V_ROWS, TA), BF16),
            jax.ShapeDtypeStruct((bsz, seq, LANES), BF16),
            jax.ShapeDtypeStruct((bsz, n_blk, LANES, TA), BF16),
        ],
        scratch_shapes=[
            pltpu.VMEM((TM + 2 * SUBLANES, D_HALF), F32),
            pltpu.VMEM((SUBLANES, LANES), F32),
        ],
        compiler_params=pltpu.CompilerParams(
            dimension_semantics=("parallel", "arbitrary"),
            vmem_limit_bytes=VMEM_LIMIT),
        name="even_in_proj",
    )(h, g_pre, w_in_all, w_f, b_f, conv_w, *consts)


def _bias_placement():
    pk = np.zeros((3, LANES, LANES), np.float32)
    pq = np.zeros((3, LANES, LANES), np.float32)
    ck = np.zeros((1, LANES), np.float32)
    cq = np.zeros((1, LANES), np.float32)
    for h in range(N_HEADS):
        for i in range(3):
            pk[i, h, EX_W * h + i] = -1.0
            cq[0, EX_W * h + i] = 1.0
            pq[i, h, EX_W * h + 3 + i] = 1.0
            ck[0, EX_W * h + 3 + i] = 1.0
    tri = np.tril(np.ones((TM, TM), np.float32))
    return (jnp.asarray(tri, BF16), jnp.asarray(np.concatenate([pk, pq], axis=2), BF16),
            jnp.asarray(ck), jnp.asarray(cq))


def _fox_kernel(qt_ref, eqt_ref, k_ref, ek_ref, vt_ref, o_ref, qs, acc, m_sc, mx_sc, al_sc, s_sc, p_sc):
    n_blk = qt_ref.shape[1]

    def query_block(i, carry):
        _fox_query_block(i, qt_ref, eqt_ref, k_ref, ek_ref, vt_ref, o_ref, qs, acc, m_sc, mx_sc, al_sc, s_sc, p_sc)
        return carry

    lax.fori_loop(0, n_blk, query_block, 0)


def _fox_query_block(i, qt_ref, eqt_ref, k_ref, ek_ref, vt_ref, o_ref, qs, acc, m_sc, mx_sc, al_sc, s_sc, p_sc):
    zeros_q = jnp.zeros((HEAD_DIM, TA), BF16)
    for h in range(N_HEADS):
        qh = qt_ref[0, i, h * HEAD_DIM:(h + 1) * HEAD_DIM, :]
        pieces = [qh, zeros_q] if h % 2 == 0 else [zeros_q, qh]
        if h > 0:
            pieces.append(jnp.zeros((EX_W * h, TA), BF16))
        pieces.append(eqt_ref[0, i, EX_W * h:EX_W * (h + 1), :])
        if h < N_HEADS - 1:
            pieces.append(jnp.zeros((LANES - EX_W * (h + 1), TA), BF16))
        qs[h] = jnp.concatenate(pieces, axis=0)
        acc[h] = jnp.zeros((V_ROWS, TA), F32)
        m_sc[h] = jnp.full((SUBLANES, TA), NEG_BIG, F32)

    key_idx = lax.broadcasted_iota(jnp.int32, (TA, TA), 0)
    qry_idx = lax.broadcasted_iota(jnp.int32, (TA, TA), 1)
    causal = key_idx <= qry_idx

    def scores(h, kk, masked):
        st = _dot(kk, qs[h])
        if masked:
            st = jnp.where(causal, st, NEG_BIG)
        s_sc[h] = st
        mx_sc[h, 0:1, :] = jnp.max(st, axis=0, keepdims=True)

    def key_pairs(j):
        r0 = pl.multiple_of(j * TA, TA)
        ekb = ek_ref[0, pl.ds(r0, TA), :]
        return [jnp.concatenate([k_ref[0, pl.ds(r0, TA), p * LANES:(p + 1) * LANES], ekb], axis=1)
                for p in range(N_HEADS // 2)]

    def probs(h):
        m_old = m_sc[h, 0:1, :]
        m_new = jnp.maximum(m_old, mx_sc[h, 0:1, :])
        al_sc[h, 0:1, :] = jnp.exp2(m_old - m_new)
        m_sc[h, 0:1, :] = m_new
        p_sc[h] = jnp.exp2(s_sc[h] - m_new).astype(BF16)

    def accumulate(j, h):
        acc[h] = al_sc[h, 0:1, :] * acc[h] + _dot(vt_ref[0, j, h * V_ROWS:(h + 1) * V_ROWS, :], p_sc[h])

    kks = key_pairs(i)
    for h in range(N_HEADS):
        scores(h, kks[h // 2], True)

    def body(j, carry):
        prev = jnp.where(j == 0, i, j - 1)
        kks = key_pairs(j)
        for h in range(N_HEADS):
            probs(h)
            scores(h, kks[h // 2], False)
            if h >= ACC_LAG:
                accumulate(prev, h - ACC_LAG)
        for h in range(N_HEADS - ACC_LAG, N_HEADS):
            accumulate(prev, h)
        return carry

    lax.fori_loop(0, i, body, 0)
    last = jnp.maximum(i - 1, 0)
    for h in range(N_HEADS):
        probs(h)
        if h >= ACC_LAG:
            accumulate(last, h - ACC_LAG)
    for h in range(N_HEADS - ACC_LAG, N_HEADS):
        accumulate(last, h)

    outs = []
    for h in range(N_HEADS):
        a = acc[h]
        outs.append(a[0:HEAD_DIM] / a[HEAD_DIM:HEAD_DIM + 1])
    o_ref[0, pl.ds(pl.multiple_of(i * TA, TA), TA), :] = jnp.concatenate(outs, axis=0).T.astype(BF16)


def _fox(qt, eqt, k, ek, vt):
    bsz, seq, _ = k.shape
    n_blk = seq // TA
    return pl.pallas_call(
        _fox_kernel,
        grid=(bsz,),
        in_specs=[
            pl.BlockSpec((1, n_blk, D_HALF, TA), lambda b: (b, 0, 0, 0)),
            pl.BlockSpec((1, n_blk, LANES, TA), lambda b: (b, 0, 0, 0)),
            pl.BlockSpec((1, seq, D_HALF), lambda b: (b, 0, 0)),
            pl.BlockSpec((1, seq, LANES), lambda b: (b, 0, 0)),
            pl.BlockSpec((1, n_blk, N_HEADS * V_ROWS, TA), lambda b: (b, 0, 0, 0)),
        ],
        out_specs=pl.BlockSpec((1, seq, D_HALF), lambda b: (b, 0, 0)),
        out_shape=jax.ShapeDtypeStruct((bsz, seq, D_HALF), BF16),
        scratch_shapes=[
            pltpu.VMEM((N_HEADS, 2 * LANES, TA), BF16),
            pltpu.VMEM((N_HEADS, V_ROWS, TA), F32),
            pltpu.VMEM((N_HEADS, SUBLANES, TA), F32),
            pltpu.VMEM((N_HEADS, SUBLANES, TA), F32),
            pltpu.VMEM((N_HEADS, SUBLANES, TA), F32),
            pltpu.VMEM((N_HEADS, TA, TA), F32),
            pltpu.VMEM((N_HEADS, TA, TA), BF16),
        ],
        compiler_params=pltpu.CompilerParams(
            dimension_semantics=("parallel",),
            vmem_limit_bytes=VMEM_LIMIT),
        name="fox_attention",
    )(qt, eqt, k, ek, vt)


def _even_out_kernel(h_ref, ya_ref, yb_ref, wo_ref, gmp_ref, gfp_ref, wi_ref, wd_ref, gfq_ref, out_ref):
    h1s, xs = [], []
    for r in range(0, TM, TM // 2):
        rows = slice(r, r + TM // 2)
        m = _dot(ya_ref[0, rows, :], wo_ref[0:D_HALF, :]) + _dot(yb_ref[0, rows, :], wo_ref[D_HALF:, :])
        h1 = h_ref[0, rows, :] + _rms(m, gmp_ref[...])
        h1s.append(h1)
        xs.append(_rms(h1, gfp_ref[...]).astype(BF16))
    out_ref[0] = _ffn(jnp.concatenate(h1s, axis=0), jnp.concatenate(xs, axis=0), wi_ref, wd_ref, gfq_ref[...])


def _ffn_specs(layer):
    return [
        _const_spec((1, D_MODEL)),
        _layer_spec((D_MODEL, 2 * D_FF), layer),
        _layer_spec((D_FF, D_MODEL), layer),
        _const_spec((1, D_MODEL)),
    ]


def _even_out(h, ya, yb, w_out_all, j, g_mix_post, layer, ffn):
    bsz, seq, _ = h.shape
    return pl.pallas_call(
        _even_out_kernel,
        grid=(bsz, seq // TM),
        in_specs=[
            _tile_spec(TM, D_MODEL),
            _tile_spec(TM, D_HALF),
            _tile_spec(TM, D_HALF),
            _layer_spec((D_MODEL, D_MODEL), j),
            _const_spec((1, D_MODEL)),
        ] + _ffn_specs(layer),
        out_specs=_tile_spec(TM, D_MODEL),
        out_shape=jax.ShapeDtypeStruct(h.shape, F32),
        compiler_params=pltpu.CompilerParams(
            dimension_semantics=("parallel", "parallel"),
            vmem_limit_bytes=VMEM_LIMIT),
        name="even_out_ffn",
    )(h, ya, yb, w_out_all, g_mix_post, *ffn)


def _odd_kernel(h_ref, gpre_ref, w1_ref, b1_ref, wdw_ref, bdw_ref, lng_ref, lnb_ref, w2_ref, b2_ref,
                gmp_ref, gfp_ref, wi_ref, wd_ref, gfq_ref, out_ref, ubuf, cbuf):
    t = pl.program_id(1)
    halo = CONV_PITCH * CONF_HALO

    @pl.when(t == 0)
    def _():
        ubuf[:, 0:halo, :] = jnp.zeros((D_MODEL // LANES, halo, LANES), F32)

    h = h_ref[0]
    x = _rms(h, gpre_ref[...]).astype(BF16)

    base = CONF_HALO - (CONF_K - 1)
    for s2 in range(D_MODEL // CONV_SLAB):
        lanes2 = slice(s2 * CONV_SLAB, (s2 + 1) * CONV_SLAB)
        glanes2 = slice(D_MODEL + s2 * CONV_SLAB, D_MODEL + (s2 + 1) * CONV_SLAB)
        a = _dot(x, w1_ref[:, lanes2]) + b1_ref[:, lanes2]
        g = _dot(x, w1_ref[:, glanes2]) + b1_ref[:, glanes2]
        u = a * _sigmoid(g)
        for half in range(CONV_SLAB // LANES):
            s = s2 * (CONV_SLAB // LANES) + half
            lanes = slice(s * LANES, (s + 1) * LANES)
            ubuf[s, pl.ds(halo, TM, stride=CONV_PITCH), :] = u[:, half * LANES:(half + 1) * LANES]
            for r in range(0, TM, CONV_ROWS):
                y = jnp.broadcast_to(bdw_ref[:, lanes], (CONV_ROWS, LANES))
                for k in range(CONF_K):
                    win = ubuf[s, pl.ds(CONV_PITCH * (r + base + k), CONV_ROWS, stride=CONV_PITCH), :]
                    y = y + wdw_ref[k:k + 1, lanes] * win
                cbuf[r:r + CONV_ROWS, lanes] = y
            ubuf[s, pl.ds(0, CONF_HALO, stride=CONV_PITCH), :] = \
                ubuf[s, pl.ds(CONV_PITCH * TM, CONF_HALO, stride=CONV_PITCH), :]

    h1s, xs = [], []
    for r in range(0, TM, TM // 2):
        rows = slice(r, r + TM // 2)
        y = cbuf[rows, :]
        mu = jnp.mean(y, axis=-1, keepdims=True)
        yc = y - mu
        var = jnp.mean(yc * yc, axis=-1, keepdims=True)
        yn = yc * lax.rsqrt(var + LN_EPS) * lng_ref[...] + lnb_ref[...]
        sw = (yn * _sigmoid(yn)).astype(BF16)
        m = _dot(sw, w2_ref[...]) + b2_ref[...]
        h1 = h[rows, :] + _rms(m, gmp_ref[...])
        h1s.append(h1)
        xs.append(_rms(h1, gfp_ref[...]).astype(BF16))
    out_ref[0] = _ffn(jnp.concatenate(h1s, axis=0), jnp.concatenate(xs, axis=0), wi_ref, wd_ref, gfq_ref[...])


def _odd_layer(h, g_pre, w1_all, b1, wdw, bdw, lng, lnb, w2_all, b2, g_mix_post, j, layer, ffn):
    bsz, seq, _ = h.shape
    return pl.pallas_call(
        _odd_kernel,
        grid=(bsz, seq // TM),
        in_specs=[
            _tile_spec(TM, D_MODEL),
            _const_spec((1, D_MODEL)),
            _layer_spec((D_MODEL, 2 * D_MODEL), j),
            _const_spec((1, 2 * D_MODEL)),
            _const_spec((CONF_K, D_MODEL)),
            _const_spec((1, D_MODEL)),
            _const_spec((1, D_MODEL)),
            _const_spec((1, D_MODEL)),
            _layer_spec((D_MODEL, D_MODEL), j),
            _const_spec((1, D_MODEL)),
            _const_spec((1, D_MODEL)),
        ] + _ffn_specs(layer),
        out_specs=_tile_spec(TM, D_MODEL),
        out_shape=jax.ShapeDtypeStruct(h.shape, F32),
        scratch_shapes=[
            pltpu.VMEM((D_MODEL // LANES, CONV_PITCH * (TM + CONF_HALO), LANES), F32),
            pltpu.VMEM((TM, D_MODEL), F32),
        ],
        compiler_params=pltpu.CompilerParams(
            dimension_semantics=("parallel", "arbitrary"),
            vmem_limit_bytes=VMEM_LIMIT),
        name="odd_conformer_ffn",
    )(h, g_pre, w1_all, b1, wdw, bdw, lng, lnb, w2_all, b2, g_mix_post, *ffn)


def _row(v):
    return v.reshape(1, -1)


def kernel(x, norm_mix_pre, norm_mix_post, norm_ffn_pre, norm_ffn_post, w_in, b_forget, w_short_conv, w_out, w_pw1, b_pw1, w_dw, b_dw, ln_g, ln_b, w_pw2, b_pw2, w_ffn_in, w_ffn_out):
    bsz, seq, d_model = x.shape
    assert d_model == D_MODEL and seq % TM == 0 and TM % TA == 0
    depth = norm_mix_pre.shape[0]
    consts = _bias_placement()
    n_main = 6 * D_HALF
    w_in_b, w_out_b, w_pw1_b, w_pw2_b = (w.astype(BF16) for w in (w_in, w_out, w_pw1, w_pw2))
    w_ffn_in_b, w_ffn_out_b = w_ffn_in.astype(BF16), w_ffn_out.astype(BF16)

    h = x
    for layer in range(depth):
        j = layer // 2
        ffn = (_row(norm_ffn_pre[layer]), w_ffn_in_b, w_ffn_out_b, _row(norm_ffn_post[layer]))
        if layer % 2 == 0:
            w_f = jnp.pad(w_in[j][:, n_main:], ((0, 0), (0, LANES - N_HEADS))).astype(BF16)
            b_f = jnp.pad(b_forget[j], (0, LANES - N_HEADS)).reshape(1, LANES)
            ya, qt, k, vt, ek, eqt = _even_in(h, _row(norm_mix_pre[layer]), w_in_b, j,
                                              w_f, b_f, w_short_conv[j], consts)
            yb = _fox(qt, eqt, k, ek, vt)
            h = _even_out(h, ya, yb, w_out_b, j, _row(norm_mix_post[layer]), layer, ffn)
        else:
            h = _odd_layer(h, _row(norm_mix_pre[layer]), w_pw1_b, _row(b_pw1[j]),
                           w_dw[j], _row(b_dw[j]), _row(ln_g[j]), _row(ln_b[j]),
                           w_pw2_b, _row(b_pw2[j]), _row(norm_mix_post[layer]), j, layer, ffn)
    return h
```
